```python
import jax, jax.numpy as jnp
from jax import lax
import numpy as np

D_MODEL = 1024
BATCH = 2
SEQ = 8192
DEPTH = 4

N_A = DEPTH // 2
N_B = DEPTH - N_A
N_VRES = max(N_A - 1, 0)
HEAD_DIM = 64
N_HEADS = D_MODEL // HEAD_DIM
D_FF = 4 * D_MODEL
DECAY_LORA = 64
AAA_LORA = 64
MV_LORA = 32
GATE_LORA = 160
Q_BLOCK = 128
N_MOD = 6
NORM_EPS = 1e-6
GN_EPS = 64e-5
L2_EPS = 1e-12

kernel_name = 'yoco_rwkv7_fox_hybrid'


def rms_norm(x, g):
    xf = x.astype(jnp.float32)
    y = xf * lax.rsqrt(jnp.mean(xf * xf, axis=-1, keepdims=True) + NORM_EPS)
    return (y * g.astype(jnp.float32)).astype(x.dtype)


def modulate(h, shift, scale):
    return h * (1.0 + scale) + shift


def token_shift(x):
    return jnp.pad(x, ((0, 0), (1, 0), (0, 0)))[:, :-1, :]


def split_heads(t):
    return t.reshape(t.shape[0], t.shape[1], N_HEADS, HEAD_DIM)


def wkv7_scan(r, decay, k, v, a_vec, b_vec):
    bsz = r.shape[0]
    xs = tuple(jnp.moveaxis(t.astype(jnp.float32), 1, 0) for t in (r, decay, k, v, a_vec, b_vec))

    def step(S, inp):
        r_t, w_t, k_t, v_t, a_t, b_t = inp
        sa = jnp.einsum('bhvk,bhk->bhv', S, a_t)
        S = S * w_t[:, :, None, :] + sa[..., :, None] * b_t[..., None, :] + v_t[..., :, None] * k_t[..., None, :]
        y = jnp.einsum('bhvk,bhk->bhv', S, r_t)
        return S, y

    S0 = jnp.zeros((bsz, N_HEADS, HEAD_DIM, HEAD_DIM), jnp.float32)
    _, ys = lax.scan(step, S0, xs)
    return jnp.moveaxis(ys, 0, 1)


def rwkv7_time_mix(h, v_first, vres, mu, wr, wk, wv, wo, w0, w1, w2, a0, a1, a2,
                   g1, g2, k_k, k_a, r_k, ln_w, ln_b):
    bsz, T, _ = h.shape
    xx = token_shift(h) - h
    xr, xw, xk, xv, xa, xg = (h + xx * mu[j] for j in range(6))
    r = xr @ wr
    w_log = -jax.nn.softplus(-(w0 + jnp.tanh(xw @ w1) @ w2)) - 0.5
    k = xk @ wk
    v = xv @ wv
    if vres is None:
        v_first = v
    else:
        v0, v1, v2 = vres
        v = v + (v_first - v) * jax.nn.sigmoid(v0 + (xv @ v1) @ v2)
    a = jax.nn.sigmoid(a0 + (xa @ a1) @ a2)
    g = jax.nn.sigmoid(xg @ g1) @ g2
    kk = split_heads(k * k_k).astype(jnp.float32)
    kk = kk / jnp.maximum(jnp.sqrt(jnp.sum(kk * kk, axis=-1, keepdims=True)), L2_EPS)
    k = k * (1.0 + (a - 1.0) * k_a)
    decay = jnp.exp(-jnp.exp(w_log.astype(jnp.float32)))
    rh, kh, vh = split_heads(r), split_heads(k), split_heads(v)
    ah = split_heads(a).astype(jnp.float32)
    y = wkv7_scan(rh, split_heads(decay), kh, vh, -kk, kk * ah)
    mean = jnp.mean(y, axis=-1, keepdims=True)
    var = jnp.mean(jnp.square(y - mean), axis=-1, keepdims=True)
    y = ((y - mean) * lax.rsqrt(var + GN_EPS)).reshape(bsz, T, D_MODEL)
    y = y * ln_w.astype(jnp.float32) + ln_b.astype(jnp.float32)
    bonus = jnp.sum((rh * kh * r_k).astype(jnp.float32), axis=-1, keepdims=True) * vh.astype(jnp.float32)
    y = (y + bonus.reshape(bsz, T, D_MODEL)).astype(h.dtype)
    return (y * g) @ wo, v_first


def shared_kv(x, shift, scale, norm_g, w_kv, f_bias, k_gain):
    h = modulate(rms_norm(x, norm_g), shift, scale)
    kvf = h @ w_kv
    k = kvf[..., :D_MODEL]
    v = kvf[..., D_MODEL:2 * D_MODEL]
    f_logit = kvf[..., 2 * D_MODEL:]
    k = rms_norm(split_heads(k), k_gain)
    log_f = jax.nn.log_sigmoid(f_logit.astype(jnp.float32) + f_bias.astype(jnp.float32))
    F = jnp.cumsum(log_f, axis=1)
    return (k.transpose(0, 2, 1, 3), split_heads(v).transpose(0, 2, 1, 3).astype(jnp.float32),
            F.transpose(0, 2, 1))


def forgetting_attention(h, k, v, F, w_qg, q_gain, w_o):
    bsz, T, _ = h.shape
    qg = h @ w_qg
    q, gate = qg[..., :D_MODEL], qg[..., D_MODEL:]
    q = rms_norm(split_heads(q), q_gain).transpose(0, 2, 1, 3)
    sm_scale = HEAD_DIM ** -0.5
    kpos = jnp.arange(T)

    def block(start):
        qb = lax.dynamic_slice_in_dim(q, start, Q_BLOCK, axis=2)
        Fq = lax.dynamic_slice_in_dim(F, start, Q_BLOCK, axis=2)
        s = jnp.einsum('bhqd,bhkd->bhqk', qb, k).astype(jnp.float32) * sm_scale
        s = s + Fq[..., :, None] - F[..., None, :]
        qpos = start + jnp.arange(Q_BLOCK)
        s = jnp.where(kpos[None, :] <= qpos[:, None], s, -jnp.inf)
        p = jax.nn.softmax(s, axis=-1)
        return jnp.einsum('bhqk,bhkd->bhqd', p, v)

    starts = jnp.arange(T // Q_BLOCK) * Q_BLOCK
    o = lax.map(block, starts)
    o = o.transpose(1, 0, 3, 2, 4).reshape(bsz, T, D_MODEL).astype(h.dtype)
    return (o * jax.nn.sigmoid(gate)) @ w_o


def sq_relu_mlp(h, w_up, w_down):
    return jnp.square(jax.nn.relu(h @ w_up)) @ w_down


def setup_inputs(seed: int = 0) -> dict:
    key = jax.random.key(seed)
    keys = iter(jax.random.split(key, 48))
    f32 = jnp.float32

    def nrm(shape, scale):
        return jax.random.normal(next(keys), shape, f32) * scale

    def unif(shape, lo, hi):
        return jax.random.uniform(next(keys), shape, f32, lo, hi)

    D = D_MODEL
    return {
        'x': nrm((BATCH, SEQ, D), 1.0),
        'c': nrm((BATCH, D), 1.0),
        'mod_w': nrm((DEPTH, D, N_MOD * D), 0.2 * D ** -0.5),
        'mod_b': nrm((DEPTH, N_MOD * D), 0.01),
        'norm_mix_g': 1.0 + nrm((DEPTH, D), 0.02),
        'norm_mlp_g': 1.0 + nrm((DEPTH, D), 0.02),
        'mlp_up': nrm((DEPTH, D, D_FF), D ** -0.5),
        'mlp_down': nrm((DEPTH, D_FF, D), D_FF ** -0.5),
        'rw_mu': unif((N_A, 6, D), 0.0, 1.0),
        'rw_wr': nrm((N_A, D, D), D ** -0.5),
        'rw_wk': nrm((N_A, D, D), D ** -0.5),
        'rw_wv': nrm((N_A, D, D), D ** -0.5),
        'rw_wo': nrm((N_A, D, D), D ** -0.5),
        'rw_w0': unif((N_A, D), -6.0, 0.0),
        'rw_w1': nrm((N_A, D, DECAY_LORA), D ** -0.5),
        'rw_w2': nrm((N_A, DECAY_LORA, D), 0.1 * DECAY_LORA ** -0.5),
        'rw_a0': nrm((N_A, D), 0.1),
        'rw_a1': nrm((N_A, D, AAA_LORA), D ** -0.5),
        'rw_a2': nrm((N_A, AAA_LORA, D), 0.1 * AAA_LORA ** -0.5),
        'rw_g1': nrm((N_A, D, GATE_LORA), D ** -0.5),
        'rw_g2': nrm((N_A, GATE_LORA, D), GATE_LORA ** -0.5),
        'rw_kk': 0.85 + nrm((N_A, D), 0.02),
        'rw_ka': 1.0 + nrm((N_A, D), 0.02),
        'rw_rk': nrm((N_A, N_HEADS, HEAD_DIM), 0.1),
        'rw_lnw': 1.0 + nrm((N_A, D), 0.02),
        'rw_lnb': nrm((N_A, D), 0.01),
        'rw_v0': 1.0 + nrm((N_VRES, D), 0.02),
        'rw_v1': nrm((N_VRES, D, MV_LORA), D ** -0.5),
        'rw_v2': nrm((N_VRES, MV_LORA, D), 0.1 * MV_LORA ** -0.5),
        'kv_norm_g': 1.0 + nrm((D,), 0.02),
        'kv_mod_w': nrm((D, 2 * D), 0.2 * D ** -0.5),
        'kv_mod_b': nrm((2 * D,), 0.01),
        'kv_w': nrm((D, 2 * D + N_HEADS), D ** -0.5),
        'kv_fb': unif((N_HEADS,), 1.0, 5.0),
        'kv_kg': 1.0 + nrm((HEAD_DIM,), 0.02),
        'fx_wqg': nrm((N_B, D, 2 * D), D ** -0.5),
        'fx_qg': 1.0 + nrm((N_B, HEAD_DIM), 0.02),
        'fx_wo': nrm((N_B, D, D), D ** -0.5),
        'final_g': 1.0 + nrm((D,), 0.02),
    }


def reference(x, c, mod_w, mod_b, norm_mix_g, norm_mlp_g, mlp_up, mlp_down,
              rw_mu, rw_wr, rw_wk, rw_wv, rw_wo, rw_w0, rw_w1, rw_w2, rw_a0, rw_a1, rw_a2,
              rw_g1, rw_g2, rw_kk, rw_ka, rw_rk, rw_lnw, rw_lnb, rw_v0, rw_v1, rw_v2,
              kv_norm_g, kv_mod_w, kv_mod_b, kv_w, kv_fb, kv_kg,
              fx_wqg, fx_qg, fx_wo, final_g):
    c_act = jax.nn.silu(c)
    v_first = None
    k_sh = v_sh = F_sh = None
    for i in range(DEPTH):
        mod = (c_act @ mod_w[i] + mod_b[i])[:, None, :]
        sh1, sc1, gt1, sh2, sc2, gt2 = jnp.split(mod, N_MOD, axis=-1)
        if i == N_A:
            kvm = (c_act @ kv_mod_w + kv_mod_b)[:, None, :]
            kv_shift, kv_scale = jnp.split(kvm, 2, axis=-1)
            k_sh, v_sh, F_sh = shared_kv(x, kv_shift, kv_scale, kv_norm_g, kv_w, kv_fb, kv_kg)
        h = modulate(rms_norm(x, norm_mix_g[i]), sh1, sc1)
        if i < N_A:
            vres = None if i == 0 else (rw_v0[i - 1], rw_v1[i - 1], rw_v2[i - 1])
            y, v_first = rwkv7_time_mix(h, v_first, vres, rw_mu[i], rw_wr[i], rw_wk[i], rw_wv[i], rw_wo[i],
                                        rw_w0[i], rw_w1[i], rw_w2[i], rw_a0[i], rw_a1[i], rw_a2[i],
                                        rw_g1[i], rw_g2[i], rw_kk[i], rw_ka[i], rw_rk[i],
                                        rw_lnw[i], rw_lnb[i])
        else:
            j = i - N_A
            y = forgetting_attention(h, k_sh, v_sh, F_sh, fx_wqg[j], fx_qg[j], fx_wo[j])
        x = x + (1.0 + gt1) * y
        h = modulate(rms_norm(x, norm_mlp_g[i]), sh2, sc2)
        x = x + (1.0 + gt2) * sq_relu_mlp(h, mlp_up[i], mlp_down[i])
    return rms_norm(x, final_g)
```

```python
import functools

import jax
import jax.numpy as jnp
from jax import lax
from jax.experimental import pallas as pl
from jax.experimental.pallas import tpu as pltpu

F32 = jnp.float32
BF16 = jnp.bfloat16

HEAD_DIM = 64
N_MOD = 6
NORM_EPS = 1e-6
GN_EPS = 64e-5
L2_EPS = 1e-12
WKV_CHUNK = 64
LANE = 128
SUBLANE = 8
VMEM_LIMIT = 56 * 1024 * 1024


def _bdot(a, b):
    return jnp.dot(a.astype(BF16), b.astype(BF16), preferred_element_type=F32)


def _bdot_nt(a, b):
    return lax.dot_general(a.astype(BF16), b.astype(BF16), (((1,), (1,)), ((), ())),
                           preferred_element_type=F32)


def _bdot_tn(a, b):
    return lax.dot_general(a.astype(BF16), b.astype(BF16), (((0,), (0,)), ((), ())),
                           preferred_element_type=F32)


def _fdot(a, b):
    return jnp.dot(a, b, preferred_element_type=F32, precision=lax.Precision.HIGHEST)


def _rms_mod(x, g, shift, scale):
    ms = jnp.mean(x * x, axis=-1, keepdims=True)
    y = x * lax.rsqrt(ms + NORM_EPS) * g
    return y * (1.0 + scale) + shift


def _const_spec(shape):
    n = len(shape)
    return pl.BlockSpec(shape, lambda *_: (0,) * n, pipeline_mode=pl.Buffered(1))


def _params(sem):
    return pltpu.CompilerParams(dimension_semantics=sem, vmem_limit_bytes=VMEM_LIMIT)


def _mod_kernel(c_ref, w_ref, b_ref, o_ref):
    c = c_ref[...]
    ca = c * jax.nn.sigmoid(c)
    o_ref[0] = _fdot(ca, w_ref[0]) + b_ref[0]


def _modulation(c_pad, w, b, tn=1024):
    L, D, M = w.shape
    return pl.pallas_call(
        _mod_kernel,
        out_shape=jax.ShapeDtypeStruct((L, SUBLANE, M), F32),
        grid=(L, M // tn),
        in_specs=[pl.BlockSpec((SUBLANE, D), lambda l, j: (0, 0)),
                  pl.BlockSpec((1, D, tn), lambda l, j: (l, 0, j)),
                  pl.BlockSpec((1, 1, tn), lambda l, j: (l, 0, j))],
        out_specs=pl.BlockSpec((1, SUBLANE, tn), lambda l, j: (l, 0, j)),
        compiler_params=_params(("parallel", "parallel")),
        name="modulation",
    )(c_pad, w, b.reshape(L, 1, M))


def _rwkv_pre_kernel(has_vres, tiles_per_batch, *refs):
    if has_vres:
        (x_ref, xp_ref, mod_ref, g_ref, mu_ref, wr_ref, wk_ref, wv_ref, w0_ref, w1_ref, w2_ref,
         a0_ref, a1_ref, a2_ref, g1_ref, g2_ref, kk_ref, ka_ref,
         vf_ref, v0_ref, v1_ref, v2_ref,
         r_out, k_out, v_out, lw_out, kk_out, a_out, g_out) = refs
    else:
        (x_ref, xp_ref, mod_ref, g_ref, mu_ref, wr_ref, wk_ref, wv_ref, w0_ref, w1_ref, w2_ref,
         a0_ref, a1_ref, a2_ref, g1_ref, g2_ref, kk_ref, ka_ref,
         r_out, k_out, v_out, lw_out, kk_out, a_out, g_out) = refs
    i = pl.program_id(0)
    shift, scale = mod_ref[0, 0], mod_ref[0, 1]
    g = g_ref[...]
    h = _rms_mod(x_ref[...], g, shift, scale)
    tm = h.shape[0]
    h_last = _rms_mod(xp_ref[SUBLANE - 1:SUBLANE, :], g, shift, scale)
    h_last = jnp.where(i % tiles_per_batch == 0, 0.0, h_last)
    row = lax.broadcasted_iota(jnp.int32, (tm, 1), 0)
    h_prev = jnp.where(row == 0, h_last, pltpu.roll(h, 1, axis=0))
    xx = h_prev - h
    xr, xw, xk, xv, xa, xg = (h + xx * mu_ref[j:j + 1, :] for j in range(6))

    r = _bdot(xr, wr_ref[...])
    k = _bdot(xk, wk_ref[...])
    v = _bdot(xv, wv_ref[...])
    w_log = -jax.nn.softplus(-(w0_ref[...] + _bdot(jnp.tanh(_bdot(xw, w1_ref[...])), w2_ref[...]))) - 0.5
    if has_vres:
        mix = jax.nn.sigmoid(v0_ref[...] + _bdot(_bdot(xv, v1_ref[...]), v2_ref[...]))
        v = v + (vf_ref[...] - v) * mix
    a = jax.nn.sigmoid(a0_ref[...] + _bdot(_bdot(xa, a1_ref[...]), a2_ref[...]))
    gate = _bdot(jax.nn.sigmoid(_bdot(xg, g1_ref[...])), g2_ref[...])

    r_out[...] = r
    k_out[...] = k * (1.0 + (a - 1.0) * ka_ref[...])
    v_out[...] = v
    lw_out[...] = -jnp.exp(w_log)
    kk_out[...] = k * kk_ref[...]
    a_out[...] = a
    g_out[...] = gate


def _rwkv_pre(x, mod, norm_g, p, v_first, T, tm=256):
    N, D = x.shape
    tpb = T // tm
    has_vres = v_first is not None
    tile = pl.BlockSpec((tm, D), lambda i: (i, 0))
    prev = pl.BlockSpec((SUBLANE, D), lambda i: (jnp.maximum(i * (tm // SUBLANE) - 1, 0), 0))
    mods = pl.BlockSpec((1, N_MOD, 1, D), lambda i: (i // tpb, 0, 0, 0))
    vec = _const_spec((1, D))
    args = [x, x, mod, norm_g, p["mu"], p["wr"], p["wk"], p["wv"], p["w0"], p["w1"], p["w2"],
            p["a0"], p["a1"], p["a2"], p["g1"], p["g2"], p["kk"], p["ka"]]
    specs = [tile, prev, mods, vec, _const_spec(p["mu"].shape)]
    specs += [_const_spec(a.shape) for a in args[5:]]
    if has_vres:
        args += [v_first, p["v0"], p["v1"], p["v2"]]
        specs += [tile, vec, _const_spec(p["v1"].shape), _const_spec(p["v2"].shape)]
    out = jax.ShapeDtypeStruct((N, D), F32)
    return pl.pallas_call(
        functools.partial(_rwkv_pre_kernel, has_vres, tpb),
        out_shape=(out,) * 7,
        grid=(N // tm,),
        in_specs=specs,
        out_specs=(tile,) * 7,
        compiler_params=_params(("parallel",)),
        name="rwkv_pre",
    )(*args)


def _unit_lower_inverse(L, blk8, steps):
    C = L.shape[0]
    row = lax.broadcasted_iota(jnp.int32, (C, C), 0)
    col = lax.broadcasted_iota(jnp.int32, (C, C), 1)
    eye = (row == col).astype(F32)
    X = jnp.where(blk8, L, 0.0)
    inv = eye + X
    for _ in range(2):
        X = _bdot(X, X)
        inv = inv + _bdot(X, inv)
    for inner, outer in steps:
        off = jnp.where(outer, jnp.where(inner, 0.0, L), 0.0)
        inv = inv + _bdot(inv, _bdot(off, inv))
    return inv


def _wkv_prep_kernel(r_ref, k_ref, v_ref, lw_ref, kk_ref, a_ref, m_out, n_out, rp_out, yl_out):
    C = WKV_CHUNK
    tb = r_ref.shape[0]
    row = lax.broadcasted_iota(jnp.int32, (C, C), 0)
    col = lax.broadcasted_iota(jnp.int32, (C, C), 1)
    tri_incl = row >= col
    tri_strict = row > col
    tri_ones = tri_incl.astype(F32)
    eye = row == col

    def same_block(s):
        return (row ^ col) < s
    blk8 = same_block(8)
    steps = []
    s = 8
    while s < C:
        steps.append((same_block(s), same_block(2 * s)))
        s *= 2

    for ci in range(tb // C):
        rows = slice(ci * C, (ci + 1) * C)
        outs = ([], [], [], [])
        for hh in range(LANE // HEAD_DIM):
            lanes = slice(hh * HEAD_DIM, (hh + 1) * HEAD_DIM)
            r, k, v = r_ref[rows, lanes], k_ref[rows, lanes], v_ref[rows, lanes]
            lw, kkr, a = lw_ref[rows, lanes], kk_ref[rows, lanes], a_ref[rows, lanes]
            kk = kkr / jnp.maximum(jnp.sqrt(jnp.sum(kkr * kkr, axis=-1, keepdims=True)), L2_EPS)
            avec, bvec = -kk, kk * a
            cw = _fdot(tri_ones, lw)
            cw_end = cw[C - 1:C, :]
            w_incl = jnp.exp(cw)
            w_inv = jnp.exp(-cw)
            w_rest = jnp.exp(cw_end - cw)
            at = avec * jnp.exp(cw - lw)
            rt = r * w_incl
            bt, kt = bvec * w_inv, k * w_inv
            bh, kh = bvec * w_rest, k * w_rest
            a_ab = jnp.where(tri_strict, _bdot_nt(at, bt), 0.0)
            a_ak = jnp.where(tri_strict, _bdot_nt(at, kt), 0.0)
            a_rb = jnp.where(tri_incl, _bdot_nt(rt, bt), 0.0)
            a_rk = jnp.where(tri_incl, _bdot_nt(rt, kt), 0.0)
            inv = _unit_lower_inverse(a_ab, blk8, steps)
            ap = _bdot(inv, at)
            ul = _bdot(inv, _bdot(a_ak, v))
            outs[0].append(jnp.where(eye, jnp.exp(cw_end), 0.0) + _bdot_tn(bh, ap))
            outs[1].append(_bdot_tn(bh, ul) + _bdot_tn(kh, v))
            outs[2].append(rt + _bdot(a_rb, ap))
            outs[3].append(_bdot(a_rb, ul) + _bdot(a_rk, v))
        for ref, parts in zip((m_out, n_out, rp_out, yl_out), outs):
            ref[rows, :] = jnp.concatenate(parts, axis=-1)


def _wkv_prep(r, k, v, lw, kk, a, tb=256):
    N, D = r.shape
    blk = pl.BlockSpec((tb, LANE), lambda i, p: (i, p))
    out = jax.ShapeDtypeStruct((N, D), F32)
    return pl.pallas_call(
        _wkv_prep_kernel,
        out_shape=(out,) * 4,
        grid=(N // tb, D // LANE),
        in_specs=[blk] * 6,
        out_specs=(blk,) * 4,
        compiler_params=_params(("parallel", "parallel")),
        name="wkv_prep",
    )(r, k, v, lw, kk, a)


def _wkv_scan_kernel(m_ref, n_ref, rp_ref, yl_ref, r_ref, k_ref, v_ref, rk_ref, lnw_ref, lnb_ref,
                     y_out, s_ref):
    C = WKV_CHUNK
    tb = m_ref.shape[0]

    @pl.when(pl.program_id(2) == 0)
    def _():
        s_ref[...] = jnp.zeros_like(s_ref)

    for hh in range(LANE // HEAD_DIM):
        lanes = slice(hh * HEAD_DIM, (hh + 1) * HEAD_DIM)
        state = s_ref[:, lanes]
        for ci in range(tb // C):
            rows = slice(ci * C, (ci + 1) * C)
            y = _fdot(rp_ref[rows, lanes], state) + yl_ref[rows, lanes]
            state = _fdot(m_ref[rows, lanes], state) + n_ref[rows, lanes]
            mean = jnp.mean(y, axis=-1, keepdims=True)
            yc = y - mean
            var = jnp.mean(yc * yc, axis=-1, keepdims=True)
            yn = yc * lax.rsqrt(var + GN_EPS) * lnw_ref[:, lanes] + lnb_ref[:, lanes]
            bonus = jnp.sum(r_ref[rows, lanes] * k_ref[rows, lanes] * rk_ref[:, lanes],
                            axis=-1, keepdims=True) * v_ref[rows, lanes]
            y_out[rows, lanes] = yn + bonus
        s_ref[:, lanes] = state


def _wkv_scan(m, n, rp, yl, r, k, v, rk, lnw, lnb, B, tb=512):
    N, D = m.shape
    nt = N // B // tb
    blk = pl.BlockSpec((tb, LANE), lambda b, p, t: (b * nt + t, p))
    vec = pl.BlockSpec((1, LANE), lambda b, p, t: (0, p))
    return pl.pallas_call(
        _wkv_scan_kernel,
        out_shape=jax.ShapeDtypeStruct((N, D), F32),
        grid=(B, D // LANE, nt),
        in_specs=[blk] * 7 + [vec] * 3,
        out_specs=blk,
        scratch_shapes=[pltpu.VMEM((HEAD_DIM, LANE), F32)],
        compiler_params=_params(("parallel", "parallel", "arbitrary")),
        name="wkv_scan",
    )(m, n, rp, yl, r, k, v, rk, lnw, lnb)


def _post_mlp_kernel(final, *refs):
    if final:
        (x_ref, y_ref, gm_ref, mod_ref, wo_ref, g_ref, up_ref, down_ref, fg_ref, o_ref) = refs
    else:
        (x_ref, y_ref, gm_ref, mod_ref, wo_ref, g_ref, up_ref, down_ref, o_ref) = refs
    gt1, sh2, sc2, gt2 = mod_ref[0, 2], mod_ref[0, 3], mod_ref[0, 4], mod_ref[0, 5]
    mix = _bdot(y_ref[...] * gm_ref[...], wo_ref[...])
    x = x_ref[...] + (1.0 + gt1) * mix
    h = _rms_mod(x, g_ref[...], sh2, sc2)
    u = jnp.maximum(_bdot(h, up_ref[...]), 0.0)
    x = x + (1.0 + gt2) * _bdot(u * u, down_ref[...])
    if final:
        ms = jnp.mean(x * x, axis=-1, keepdims=True)
        x = x * lax.rsqrt(ms + NORM_EPS) * fg_ref[...]
    o_ref[...] = x


def _post_mlp(x, y, gmul, mod, wo, norm_g, w_up, w_down, T, final_g=None, tm=256):
    N, D = x.shape
    tpb = T // tm
    final = final_g is not None
    tile = pl.BlockSpec((tm, D), lambda i: (i, 0))
    mods = pl.BlockSpec((1, N_MOD, 1, D), lambda i: (i // tpb, 0, 0, 0))
    args = [x, y, gmul, mod, wo, norm_g, w_up, w_down]
    specs = [tile, tile, tile, mods, _const_spec(wo.shape), _const_spec((1, D)),
             _const_spec(w_up.shape), _const_spec(w_down.shape)]
    if final:
        args.append(final_g)
        specs.append(_const_spec((1, D)))
    return pl.pallas_call(
        functools.partial(_post_mlp_kernel, final),
        out_shape=jax.ShapeDtypeStruct((N, D), F32),
        grid=(N // tm,),
        in_specs=specs,
        out_specs=tile,
        compiler_params=_params(("parallel",)),
        name="post_mlp",
    )(*args)


def _head_mean_square(x, ind_ref, indt_ref):
    ss = _fdot(x * x, ind_ref[...])
    return _fdot(ss, indt_ref[...]) * (1.0 / HEAD_DIM)


def _shared_kv_kernel(tiles_per_batch, x_ref, mod_ref, g_ref, wk_ref, wv_ref, wf_ref, fb_ref, kg_ref,
                      ind_ref, indt_ref, k_out, v_out, f_out, carry_ref):
    i = pl.program_id(0)

    @pl.when(i % tiles_per_batch == 0)
    def _():
        carry_ref[...] = jnp.zeros_like(carry_ref)

    h = _rms_mod(x_ref[...], g_ref[...], mod_ref[0, 0], mod_ref[0, 1])
    tm = h.shape[0]
    k = _bdot(h, wk_ref[...])
    k = k * lax.rsqrt(_head_mean_square(k, ind_ref, indt_ref) + NORM_EPS) * kg_ref[...]
    k_out[...] = k.astype(BF16)
    v_out[...] = _bdot(h, wv_ref[...]).astype(BF16)
    log_f = jax.nn.log_sigmoid(_bdot(h, wf_ref[...]) + fb_ref[...])
    row = lax.broadcasted_iota(jnp.int32, (tm, tm), 0)
    col = lax.broadcasted_iota(jnp.int32, (tm, tm), 1)
    cum = _fdot((row >= col).astype(F32), log_f) + carry_ref[...]
    f_out[...] = cum
    carry_ref[...] = cum[tm - 1:tm, :]


def _shared_kv(x, mod, norm_g, wk, wv, wf, fb, kg, ind, indt, T, tm=256):
    N, D = x.shape
    tpb = T // tm
    tile = pl.BlockSpec((tm, D), lambda i: (i, 0))
    mods = pl.BlockSpec((1, 2, 1, D), lambda i: (i // tpb, 0, 0, 0))
    return pl.pallas_call(
        functools.partial(_shared_kv_kernel, tpb),
        out_shape=(jax.ShapeDtypeStruct((N, D), BF16), jax.ShapeDtypeStruct((N, D), BF16),
                   jax.ShapeDtypeStruct((N, LANE), F32)),
        grid=(N // tm,),
        in_specs=[tile, mods, _const_spec((1, D)), _const_spec(wk.shape), _const_spec(wv.shape),
                  _const_spec(wf.shape), _const_spec((1, LANE)), _const_spec((1, D)),
                  _const_spec(ind.shape), _const_spec(indt.shape)],
        out_specs=(tile, tile, pl.BlockSpec((tm, LANE), lambda i: (i, 0))),
        scratch_shapes=[pltpu.VMEM((1, LANE), F32)],
        compiler_params=_params(("arbitrary",)),
        name="shared_kv",
    )(x, mod, norm_g, wk, wv, wf, fb, kg, ind, indt)


def _fox_q_kernel(x_ref, mod_ref, g_ref, wq_ref, wg_ref, qg_ref, ind_ref, indt_ref, q_out, sg_out):
    h = _rms_mod(x_ref[...], g_ref[...], mod_ref[0, 0], mod_ref[0, 1])
    q = _bdot(h, wq_ref[...])
    q = q * lax.rsqrt(_head_mean_square(q, ind_ref, indt_ref) + NORM_EPS) * qg_ref[...]
    q_out[...] = (q * (HEAD_DIM ** -0.5)).astype(BF16)
    sg_out[...] = jax.nn.sigmoid(_bdot(h, wg_ref[...]))


def _fox_q(x, mod, norm_g, wq, wg, qg, ind, indt, T, tm=256):
    N, D = x.shape
    tpb = T // tm
    tile = pl.BlockSpec((tm, D), lambda i: (i, 0))
    mods = pl.BlockSpec((1, N_MOD, 1, D), lambda i: (i // tpb, 0, 0, 0))
    return pl.pallas_call(
        _fox_q_kernel,
        out_shape=(jax.ShapeDtypeStruct((N, D), BF16), jax.ShapeDtypeStruct((N, D), F32)),
        grid=(N // tm,),
        in_specs=[tile, mods, _const_spec((1, D)), _const_spec(wq.shape), _const_spec(wg.shape),
                  _const_spec((1, D)), _const_spec(ind.shape), _const_spec(indt.shape)],
        out_specs=(tile, tile),
        compiler_params=_params(("parallel",)),
        name="fox_q",
    )(x, mod, norm_g, wq, wg, qg, ind, indt)


def _fox_attn_kernel(tq, q_ref, k_ref, v_ref, fq_ref, ft_ref, o_ref):
    p = pl.program_id(1)
    qi = pl.program_id(2)
    lane_id = lax.broadcasted_iota(jnp.int32, (tq, LANE), 1)
    row = lax.broadcasted_iota(jnp.int32, (tq, tq), 0)
    col = lax.broadcasted_iota(jnp.int32, (tq, tq), 1)
    causal = row >= col
    outs = []
    for hh in range(LANE // HEAD_DIM):
        lanes = slice(hh * HEAD_DIM, (hh + 1) * HEAD_DIM)
        head = p * (LANE // HEAD_DIM) + hh
        q = q_ref[:, lanes]
        f_q = jnp.sum(jnp.where(lane_id == head, fq_ref[...], 0.0), axis=-1, keepdims=True)

        def scores(j):
            start = pl.multiple_of(j * tq, tq)
            k = k_ref[pl.ds(start, tq), lanes]
            f_k = ft_ref[0, pl.ds(head, 1), pl.ds(start, tq)]
            return _bdot_nt(q, k) + (f_q - f_k), v_ref[pl.ds(start, tq), lanes]

        def update(carry, s, v):
            m, l, acc = carry
            m_new = jnp.maximum(m, jnp.max(s, axis=-1, keepdims=True))
            alpha = jnp.exp(m - m_new)
            e = jnp.exp(s - m_new)
            l = alpha * l + jnp.sum(e, axis=-1, keepdims=True)
            acc = alpha * acc + _bdot(e, v)
            return m_new, l, acc

        def body(j, carry):
            s, v = scores(j)
            return update(carry, s, v)

        init = (jnp.full((tq, 1), -jnp.inf, F32), jnp.zeros((tq, 1), F32), jnp.zeros((tq, HEAD_DIM), F32))
        carry = lax.fori_loop(0, qi, body, init)
        s, v = scores(qi)
        m, l, acc = update(carry, jnp.where(causal, s, -jnp.inf), v)
        outs.append(acc / l)
    o_ref[...] = jnp.concatenate(outs, axis=-1)


def _fox_attn(q, k, v, f, ft, B, tq=256):
    N, D = q.shape
    T = N // B
    nq = T // tq
    qblk = pl.BlockSpec((tq, LANE), lambda b, p, i: (b * nq + i, p))
    kvblk = pl.BlockSpec((T, LANE), lambda b, p, i: (b, p))
    return pl.pallas_call(
        functools.partial(_fox_attn_kernel, tq),
        out_shape=jax.ShapeDtypeStruct((N, D), F32),
        grid=(B, D // LANE, nq),
        in_specs=[qblk, kvblk, kvblk,
                  pl.BlockSpec((tq, LANE), lambda b, p, i: (b * nq + i, 0)),
                  pl.BlockSpec((1, ft.shape[1], T), lambda b, p, i: (b, 0, 0))],
        out_specs=qblk,
        compiler_params=_params(("parallel", "parallel", "arbitrary")),
        name="fox_attn",
    )(q, k, v, f, ft)


def kernel(x, c, mod_w, mod_b, norm_mix_g, norm_mlp_g, mlp_up, mlp_down, rw_mu, rw_wr, rw_wk, rw_wv, rw_wo, rw_w0, rw_w1, rw_w2, rw_a0, rw_a1, rw_a2, rw_g1, rw_g2, rw_kk, rw_ka, rw_rk, rw_lnw, rw_lnb, rw_v0, rw_v1, rw_v2, kv_norm_g, kv_mod_w, kv_mod_b, kv_w, kv_fb, kv_kg, fx_wqg, fx_qg, fx_wo, final_g):
    B, T, D = x.shape
    depth = mod_w.shape[0]
    n_a = rw_wr.shape[0]
    H = D // HEAD_DIM
    N = B * T
    bf = lambda w: w.astype(BF16)
    row = lambda vct: vct.reshape(1, -1)

    c_pad = jnp.pad(c, ((0, SUBLANE - B), (0, 0)))
    mods = _modulation(c_pad, mod_w, mod_b)[:, :B].reshape(depth, B, N_MOD, 1, D)
    kv_mod = _modulation(c_pad, kv_mod_w[None], kv_mod_b[None])[0, :B].reshape(B, 2, 1, D)

    head_of = jnp.arange(D, dtype=jnp.int32) // HEAD_DIM
    ind = (head_of[:, None] == jnp.arange(LANE, dtype=jnp.int32)[None, :]).astype(F32)
    indt = ind.T

    xf = x.reshape(N, D)
    v_first = None
    k_sh = v_sh = f_sh = ft_sh = None
    for i in range(depth):
        if i < n_a:
            p = dict(mu=rw_mu[i], wr=bf(rw_wr[i]), wk=bf(rw_wk[i]), wv=bf(rw_wv[i]),
                     w0=row(rw_w0[i]), w1=bf(rw_w1[i]), w2=bf(rw_w2[i]),
                     a0=row(rw_a0[i]), a1=bf(rw_a1[i]), a2=bf(rw_a2[i]),
                     g1=bf(rw_g1[i]), g2=bf(rw_g2[i]), kk=row(rw_kk[i]), ka=row(rw_ka[i]))
            if i > 0:
                p.update(v0=row(rw_v0[i - 1]), v1=bf(rw_v1[i - 1]), v2=bf(rw_v2[i - 1]))
            r, k, v, lw, kk, a, gate = _rwkv_pre(xf, mods[i], row(norm_mix_g[i]), p,
                                                 v_first if i > 0 else None, T)
            if i == 0:
                v_first = v
            m, n, rp, yl = _wkv_prep(r, k, v, lw, kk, a)
            y = _wkv_scan(m, n, rp, yl, r, k, v, row(rw_rk[i]), row(rw_lnw[i]), row(rw_lnb[i]), B)
            gmul, wo = gate, bf(rw_wo[i])
        else:
            j = i - n_a
            if j == 0:
                wf = jnp.pad(kv_w[:, 2 * D:], ((0, 0), (0, LANE - H)))
                fb = jnp.pad(kv_fb, (0, LANE - H)).reshape(1, LANE)
                k_sh, v_sh, f_sh = _shared_kv(xf, kv_mod, row(kv_norm_g), bf(kv_w[:, :D]),
                                              bf(kv_w[:, D:2 * D]), bf(wf), fb,
                                              row(jnp.tile(kv_kg, H)), ind, indt, T)
                ft_sh = f_sh[:, :H].reshape(B, T, H).transpose(0, 2, 1)
            q, gmul = _fox_q(xf, mods[i], row(norm_mix_g[i]), bf(fx_wqg[j][:, :D]), bf(fx_wqg[j][:, D:]),
                             row(jnp.tile(fx_qg[j], H)), ind, indt, T)
            y = _fox_attn(q, k_sh, v_sh, f_sh, ft_sh, B)
            wo = bf(fx_wo[j])
        xf = _post_mlp(xf, y, gmul, mods[i], wo, row(norm_mlp_g[i]), bf(mlp_up[i]), bf(mlp_down[i]), T,
                       final_g=row(final_g) if i == depth - 1 else None)
    return xf.reshape(B, T, D)
```

```python
import functools

import jax
import jax.numpy as jnp
from jax import lax
from jax.experimental import pallas as pl
from jax.experimental.pallas import tpu as pltpu

F32 = jnp.float32
BF16 = jnp.bfloat16

HEAD_DIM = 64
N_MOD = 6
NORM_EPS = 1e-6
GN_EPS = 64e-5
L2_EPS = 1e-12
WKV_CHUNK = 64
LANE = 128
SUBLANE = 8
VMEM_LIMIT = 56 * 1024 * 1024
TOKEN_TILE = 256
LOG2E = 1.4426950408889634


def _bdot(a, b):
    return jnp.dot(a.astype(BF16), b.astype(BF16), preferred_element_type=F32)


def _bdot_nt(a, b):
    return lax.dot_general(a.astype(BF16), b.astype(BF16), (((1,), (1,)), ((), ())),
                           preferred_element_type=F32)


def _bdot_tn(a, b):
    return lax.dot_general(a.astype(BF16), b.astype(BF16), (((0,), (0,)), ((), ())),
                           preferred_element_type=F32)


def _fdot(a, b):
    return jnp.dot(a, b, preferred_element_type=F32, precision=lax.Precision.HIGHEST)


def _rms_mod(x, g, shift, scale):
    ms = jnp.mean(x * x, axis=-1, keepdims=True)
    y = x * lax.rsqrt(ms + NORM_EPS) * g
    return y * (1.0 + scale) + shift


def _const_spec(shape):
    n = len(shape)
    return pl.BlockSpec(shape, lambda *_: (0,) * n, pipeline_mode=pl.Buffered(1))


def _params(sem):
    return pltpu.CompilerParams(dimension_semantics=sem, vmem_limit_bytes=VMEM_LIMIT)


def _mod_kernel(c_ref, w_ref, b_ref, o_ref):
    c = c_ref[...]
    ca = c * jax.nn.sigmoid(c)
    o_ref[0] = _fdot(ca, w_ref[0]) + b_ref[0]


def _modulation(c_pad, w, b, tn=1024):
    L, D, M = w.shape
    return pl.pallas_call(
        _mod_kernel,
        out_shape=jax.ShapeDtypeStruct((L, SUBLANE, M), F32),
        grid=(L, M // tn),
        in_specs=[pl.BlockSpec((SUBLANE, D), lambda l, j: (0, 0)),
                  pl.BlockSpec((1, D, tn), lambda l, j: (l, 0, j)),
                  pl.BlockSpec((1, 1, tn), lambda l, j: (l, 0, j))],
        out_specs=pl.BlockSpec((1, SUBLANE, tn), lambda l, j: (l, 0, j)),
        compiler_params=_params(("parallel", "parallel")),
        name="modulation",
    )(c_pad, w, b.reshape(L, 1, M))


def _rwkv_pre_kernel(has_vres, tiles_per_batch, *refs):
    if has_vres:
        (x_ref, xp_ref, mod_ref, g_ref, mu_ref, wr_ref, wk_ref, wv_ref, w0_ref, w1_ref, w2_ref,
         a0_ref, a1_ref, a2_ref, g1_ref, g2_ref, kk_ref, ka_ref,
         vf_ref, v0_ref, v1_ref, v2_ref,
         r_out, k_out, v_out, lw_out, kk_out, a_out, g_out) = refs
    else:
        (x_ref, xp_ref, mod_ref, g_ref, mu_ref, wr_ref, wk_ref, wv_ref, w0_ref, w1_ref, w2_ref,
         a0_ref, a1_ref, a2_ref, g1_ref, g2_ref, kk_ref, ka_ref,
         r_out, k_out, v_out, lw_out, kk_out, a_out, g_out) = refs
    i = pl.program_id(0)
    shift, scale = mod_ref[0, 0], mod_ref[0, 1]
    g = g_ref[...]
    h = _rms_mod(x_ref[...], g, shift, scale)
    tm = h.shape[0]
    h_last = _rms_mod(xp_ref[SUBLANE - 1:SUBLANE, :], g, shift, scale)
    h_last = jnp.where(i % tiles_per_batch == 0, 0.0, h_last)
    row = lax.broadcasted_iota(jnp.int32, (tm, 1), 0)
    h_prev = jnp.where(row == 0, h_last, pltpu.roll(h, 1, axis=0))
    xx = h_prev - h
    xr, xw, xk, xv, xa, xg = (h + xx * mu_ref[j:j + 1, :] for j in range(6))

    r = _bdot(xr, wr_ref[...])
    k = _bdot(xk, wk_ref[...])
    v = _bdot(xv, wv_ref[...])
    w_log = -jax.nn.softplus(-(w0_ref[...] + _bdot(jnp.tanh(_bdot(xw, w1_ref[...])), w2_ref[...]))) - 0.5
    if has_vres:
        mix = jax.nn.sigmoid(v0_ref[...] + _bdot(_bdot(xv, v1_ref[...]), v2_ref[...]))
        v = v + (vf_ref[...] - v) * mix
    a = jax.nn.sigmoid(a0_ref[...] + _bdot(_bdot(xa, a1_ref[...]), a2_ref[...]))
    gate = _bdot(jax.nn.sigmoid(_bdot(xg, g1_ref[...])), g2_ref[...])

    r_out[...] = r
    k_out[...] = k * (1.0 + (a - 1.0) * ka_ref[...])
    v_out[...] = v
    lw_out[...] = -jnp.exp(w_log)
    kk_out[...] = k * kk_ref[...]
    a_out[...] = a
    g_out[...] = gate


def _rwkv_pre(x, mod, norm_g, p, v_first, T, tm=TOKEN_TILE):
    N, D = x.shape
    tpb = T // tm
    has_vres = v_first is not None
    tile = pl.BlockSpec((tm, D), lambda i: (i, 0))
    prev = pl.BlockSpec((SUBLANE, D), lambda i: (jnp.maximum(i * (tm // SUBLANE) - 1, 0), 0))
    mods = pl.BlockSpec((1, N_MOD, 1, D), lambda i: (i // tpb, 0, 0, 0))
    vec = _const_spec((1, D))
    args = [x, x, mod, norm_g, p["mu"], p["wr"], p["wk"], p["wv"], p["w0"], p["w1"], p["w2"],
            p["a0"], p["a1"], p["a2"], p["g1"], p["g2"], p["kk"], p["ka"]]
    specs = [tile, prev, mods, vec, _const_spec(p["mu"].shape)]
    specs += [_const_spec(a.shape) for a in args[5:]]
    if has_vres:
        args += [v_first, p["v0"], p["v1"], p["v2"]]
        specs += [tile, vec, _const_spec(p["v1"].shape), _const_spec(p["v2"].shape)]
    out = jax.ShapeDtypeStruct((N, D), F32)
    return pl.pallas_call(
        functools.partial(_rwkv_pre_kernel, has_vres, tpb),
        out_shape=(out,) * 7,
        grid=(N // tm,),
        in_specs=specs,
        out_specs=(tile,) * 7,
        compiler_params=_params(("parallel",)),
        name="rwkv_pre",
    )(*args)


def _blockdiag(w, diag):
    return jnp.where(diag, jnp.concatenate([w, w], axis=0), jnp.zeros((), w.dtype))


def _pair_mm(l, rs, diag):
    rhs = jnp.concatenate([_blockdiag(r.astype(BF16), diag) for r in rs], axis=1)
    return jnp.dot(l.astype(BF16), rhs, preferred_element_type=F32)


def _pair_mm_nt(l, rs, diag):
    rhs = jnp.concatenate([_blockdiag(r.astype(BF16), diag) for r in rs], axis=0)
    return lax.dot_general(l.astype(BF16), rhs, (((1,), (1,)), ((), ())), preferred_element_type=F32)


def _pair_mm_tn(l, rs, first):
    rhs = jnp.concatenate([r.astype(BF16) for r in rs], axis=1)
    full = lax.dot_general(l.astype(BF16), rhs, (((0,), (0,)), ((), ())), preferred_element_type=F32)
    return [jnp.where(first, full[:HEAD_DIM, i * LANE:(i + 1) * LANE], full[HEAD_DIM:, i * LANE:(i + 1) * LANE])
            for i in range(len(rs))]


def _pair_masks(C):
    row = lax.broadcasted_iota(jnp.int32, (C, LANE), 0)
    lane = lax.broadcasted_iota(jnp.int32, (C, LANE), 1)
    col = lane & (HEAD_DIM - 1)
    first = lane < HEAD_DIM
    row2 = lax.broadcasted_iota(jnp.int32, (LANE, LANE), 0)
    lane2 = lax.broadcasted_iota(jnp.int32, (LANE, LANE), 1)
    diag = (row2 < HEAD_DIM) == (lane2 < HEAD_DIM)
    return row, col, first, diag


def _pair_sum(x, first):
    s0 = jnp.sum(jnp.where(first, x, 0.0), axis=-1, keepdims=True)
    s1 = jnp.sum(jnp.where(first, 0.0, x), axis=-1, keepdims=True)
    return jnp.where(first, s0, s1)


def _wkv_prep_kernel(r_ref, k_ref, v_ref, lw_ref, kk_ref, a_ref, m_out, n_out, rp_out, yl_out):
    C = WKV_CHUNK
    chunks = [slice(ci * C, (ci + 1) * C) for ci in range(r_ref.shape[0] // C)]
    row, col, first, diag = _pair_masks(C)
    tri_incl, tri_strict, eye = row >= col, row > col, row == col
    cum_op = (lax.broadcasted_iota(jnp.int32, (C, C), 0) >= lax.broadcasted_iota(jnp.int32, (C, C), 1)).astype(F32)

    def same_block(s):
        return (row ^ col) < s
    levels, s = [], 8
    while s < C:
        levels.append((same_block(s), same_block(2 * s)))
        s *= 2

    at, rt, bt, kt, bh, kh, w_end, vs = [], [], [], [], [], [], [], []
    for rows in chunks:
        lw, kkr = lw_ref[rows, :], kk_ref[rows, :]
        kk = kkr / jnp.maximum(jnp.sqrt(_pair_sum(kkr * kkr, first)), L2_EPS)
        bvec = kk * a_ref[rows, :]
        cw = _fdot(cum_op, lw)
        cw_end = cw[C - 1:C, :]
        w_inv, w_rest = jnp.exp(-cw), jnp.exp(cw_end - cw)
        k = k_ref[rows, :]
        at.append(-kk * jnp.exp(cw - lw))
        rt.append(r_ref[rows, :] * jnp.exp(cw))
        bt.append(bvec * w_inv)
        kt.append(k * w_inv)
        bh.append(bvec * w_rest)
        kh.append(k * w_rest)
        w_end.append(jnp.exp(cw_end))
        vs.append(v_ref[rows, :])
    n = range(len(chunks))

    amat = [_pair_mm_nt(jnp.concatenate([at[i], rt[i]], axis=0), [bt[i], kt[i]], diag) for i in n]
    a_ab = [jnp.where(tri_strict, amat[i][:C, :LANE], 0.0) for i in n]
    a_ak = [jnp.where(tri_strict, amat[i][:C, LANE:], 0.0) for i in n]
    a_rb = [jnp.where(tri_incl, amat[i][C:, :LANE], 0.0) for i in n]
    a_rk = [jnp.where(tri_incl, amat[i][C:, LANE:], 0.0) for i in n]
    akv = [_pair_mm(a_ak[i], [vs[i]], diag) for i in n]

    blk8 = same_block(8)
    x = [jnp.where(blk8, a_ab[i], 0.0) for i in n]
    inv = [jnp.where(eye, 1.0, x[i]) for i in n]
    for _ in range(2):
        x = [_pair_mm(x[i], [x[i]], diag) for i in n]
        inv = [inv[i] + _pair_mm(x[i], [inv[i]], diag) for i in n]
    for inner, outer in levels:
        off = [jnp.where(outer, jnp.where(inner, 0.0, a_ab[i]), 0.0) for i in n]
        t = [_pair_mm(off[i], [inv[i]], diag) for i in n]
        inv = [inv[i] + _pair_mm(inv[i], [t[i]], diag) for i in n]

    sol = [_pair_mm(inv[i], [at[i], akv[i]], diag) for i in n]
    ap = [sol[i][:, :LANE] for i in n]
    ul = [sol[i][:, LANE:] for i in n]
    for i in n:
        rows = chunks[i]
        rb = _pair_mm(a_rb[i], [ap[i], ul[i]], diag)
        rp_out[rows, :] = rt[i] + rb[:, :LANE]
        yl_out[rows, :] = rb[:, LANE:] + _pair_mm(a_rk[i], [vs[i]], diag)
        bh_ap, bh_ul = _pair_mm_tn(bh[i], [ap[i], ul[i]], first)
        m_out[rows, :] = jnp.where(eye, w_end[i], 0.0) + bh_ap
        n_out[rows, :] = bh_ul + _pair_mm_tn(kh[i], [vs[i]], first)[0]


def _wkv_prep(r, k, v, lw, kk, a, tb=512):
    N, D = r.shape
    assert WKV_CHUNK == HEAD_DIM and 2 * HEAD_DIM == LANE
    blk = pl.BlockSpec((tb, LANE), lambda i, p: (i, p))
    out = jax.ShapeDtypeStruct((N, D), F32)
    return pl.pallas_call(
        _wkv_prep_kernel,
        out_shape=(out,) * 4,
        grid=(N // tb, D // LANE),
        in_specs=[blk] * 6,
        out_specs=(blk,) * 4,
        compiler_params=_params(("parallel", "parallel")),
        name="wkv_prep",
    )(r, k, v, lw, kk, a)


def _wkv_scan_kernel(m_ref, n_ref, rp_ref, yl_ref, r_ref, k_ref, v_ref, rk_ref, lnw_ref, lnb_ref,
                     y_out, s_ref):
    C = WKV_CHUNK
    B, tb, D = m_ref.shape
    _, _, first, diag = _pair_masks(C)

    @pl.when(pl.program_id(0) == 0)
    def _():
        s_ref[...] = jnp.zeros_like(s_ref)

    pairs = [(b, slice(p * LANE, (p + 1) * LANE)) for b in range(B) for p in range(D // LANE)]
    for ci in range(tb // C):
        rows = slice(ci * C, (ci + 1) * C)
        sdiag = [_blockdiag(s_ref[b, :, lanes], diag) for b, lanes in pairs]
        ys = [_fdot(rp_ref[b, rows, lanes], sd) + yl_ref[b, rows, lanes] for (b, lanes), sd in zip(pairs, sdiag)]
        for (b, lanes), sd in zip(pairs, sdiag):
            s_ref[b, :, lanes] = _fdot(m_ref[b, rows, lanes], sd) + n_ref[b, rows, lanes]
        for (b, lanes), y in zip(pairs, ys):
            yc = y - _pair_sum(y, first) * (1.0 / HEAD_DIM)
            var = _pair_sum(yc * yc, first) * (1.0 / HEAD_DIM)
            yn = yc * lax.rsqrt(var + GN_EPS) * lnw_ref[:, lanes] + lnb_ref[:, lanes]
            rk = r_ref[b, rows, lanes] * k_ref[b, rows, lanes] * rk_ref[:, lanes]
            y_out[b, rows, lanes] = yn + _pair_sum(rk, first) * v_ref[b, rows, lanes]


def _wkv_scan(m, n, rp, yl, r, k, v, rk, lnw, lnb, B, tb=128):
    N, D = m.shape
    T = N // B
    blk = pl.BlockSpec((B, tb, D), lambda t: (0, t, 0))
    vec = pl.BlockSpec((1, D), lambda t: (0, 0))
    args = [a.reshape(B, T, D) for a in (m, n, rp, yl, r, k, v)]
    return pl.pallas_call(
        _wkv_scan_kernel,
        out_shape=jax.ShapeDtypeStruct((B, T, D), F32),
        grid=(T // tb,),
        in_specs=[blk] * 7 + [vec] * 3,
        out_specs=blk,
        scratch_shapes=[pltpu.VMEM((B, HEAD_DIM, D), F32)],
        compiler_params=_params(("arbitrary",)),
        name="wkv_scan",
    )(*args, rk, lnw, lnb).reshape(N, D)


def _post_mlp_kernel(final, channel_major, *refs):
    if final:
        (x_ref, y_ref, gm_ref, mod_ref, wo_ref, g_ref, up_ref, down_ref, fg_ref, o_ref) = refs
    else:
        (x_ref, y_ref, gm_ref, mod_ref, wo_ref, g_ref, up_ref, down_ref, o_ref) = refs
    gt1, sh2, sc2, gt2 = mod_ref[0, 2], mod_ref[0, 3], mod_ref[0, 4], mod_ref[0, 5]
    gated = y_ref[...] * gm_ref[...]
    if channel_major:
        mix = _bdot_tn(gated, wo_ref[...])
    else:
        mix = _bdot(gated, wo_ref[...])
    x = x_ref[...] + (1.0 + gt1) * mix
    h = _rms_mod(x, g_ref[...], sh2, sc2)
    u = jnp.maximum(_bdot(h, up_ref[...]), 0.0)
    x = x + (1.0 + gt2) * _bdot(u * u, down_ref[...])
    if final:
        ms = jnp.mean(x * x, axis=-1, keepdims=True)
        x = x * lax.rsqrt(ms + NORM_EPS) * fg_ref[...]
    o_ref[...] = x


def _post_mlp(x, y, gmul, mod, wo, norm_g, w_up, w_down, T, channel_major, final_g=None, tm=TOKEN_TILE):
    N, D = x.shape
    tpb = T // tm
    final = final_g is not None
    tile = pl.BlockSpec((tm, D), lambda i: (i, 0))
    mixer = pl.BlockSpec((D, tm), lambda i: (0, i)) if channel_major else tile
    mods = pl.BlockSpec((1, N_MOD, 1, D), lambda i: (i // tpb, 0, 0, 0))
    args = [x, y, gmul, mod, wo, norm_g, w_up, w_down]
    specs = [tile, mixer, mixer, mods, _const_spec(wo.shape), _const_spec((1, D)),
             _const_spec(w_up.shape), _const_spec(w_down.shape)]
    if final:
        args.append(final_g)
        specs.append(_const_spec((1, D)))
    return pl.pallas_call(
        functools.partial(_post_mlp_kernel, final, channel_major),
        out_shape=jax.ShapeDtypeStruct((N, D), F32),
        grid=(N // tm,),
        in_specs=specs,
        out_specs=tile,
        compiler_params=_params(("parallel",)),
        name="post_mlp",
    )(*args)


def _shared_kv_kernel(tiles_per_batch, x_ref, mod_ref, g_ref, wk_ref, wvt_ref, wf_ref, fb_ref, kg_ref,
                      ind_ref, indt_ref, k_out, vt_out, f_out, carry_ref):
    i = pl.program_id(0)

    @pl.when(i % tiles_per_batch == 0)
    def _():
        carry_ref[...] = jnp.zeros_like(carry_ref)

    h = _rms_mod(x_ref[...], g_ref[...], mod_ref[0, 0], mod_ref[0, 1])
    tm = h.shape[0]
    k = _bdot(h, wk_ref[...])
    ss = _fdot(k * k, ind_ref[...])
    ms = _fdot(ss, indt_ref[...]) * (1.0 / HEAD_DIM)
    k_out[...] = (k * lax.rsqrt(ms + NORM_EPS) * kg_ref[...]).astype(BF16)
    vt_out[...] = _bdot_nt(wvt_ref[...], h).astype(BF16)
    log_f = jax.nn.log_sigmoid(_bdot(h, wf_ref[...]) + fb_ref[...])
    row = lax.broadcasted_iota(jnp.int32, (tm, tm), 0)
    col = lax.broadcasted_iota(jnp.int32, (tm, tm), 1)
    cum = _fdot((row >= col).astype(F32), log_f) + carry_ref[...]
    f_out[...] = cum
    carry_ref[...] = cum[tm - 1:tm, :]


def _shared_kv(x, mod, norm_g, wk, wvt, wf, fb, kg, ind, indt, T, tm=TOKEN_TILE):
    N, D = x.shape
    tpb = T // tm
    tile = pl.BlockSpec((tm, D), lambda i: (i, 0))
    mods = pl.BlockSpec((1, 2, 1, D), lambda i: (i // tpb, 0, 0, 0))
    return pl.pallas_call(
        functools.partial(_shared_kv_kernel, tpb),
        out_shape=(jax.ShapeDtypeStruct((N, D), BF16), jax.ShapeDtypeStruct((D, N), BF16),
                   jax.ShapeDtypeStruct((N, LANE), F32)),
        grid=(N // tm,),
        in_specs=[tile, mods, _const_spec((1, D)), _const_spec(wk.shape), _const_spec(wvt.shape),
                  _const_spec(wf.shape), _const_spec((1, LANE)), _const_spec((1, D)),
                  _const_spec(ind.shape), _const_spec(indt.shape)],
        out_specs=(tile, pl.BlockSpec((D, tm), lambda i: (0, i)), pl.BlockSpec((tm, LANE), lambda i: (i, 0))),
        scratch_shapes=[pltpu.VMEM((1, LANE), F32)],
        compiler_params=_params(("arbitrary",)),
        name="shared_kv",
    )(x, mod, norm_g, wk, wvt, wf, fb, kg, ind, indt)


def _fox_q_kernel(x_ref, mod_ref, g_ref, wqt_ref, wgt_ref, qg_ref, qt_out, sgt_out):
    h = _rms_mod(x_ref[...], g_ref[...], mod_ref[0, 0], mod_ref[0, 1])
    tm = h.shape[0]
    qt = _bdot_nt(wqt_ref[...], h)
    q3 = qt.reshape(qt.shape[0] // HEAD_DIM, HEAD_DIM, tm)
    q3 = q3 * lax.rsqrt(jnp.mean(q3 * q3, axis=1, keepdims=True) + NORM_EPS)
    qt_out[...] = (q3.reshape(qt.shape) * qg_ref[...] * (HEAD_DIM ** -0.5 * LOG2E)).astype(BF16)
    sgt_out[...] = jax.nn.sigmoid(_bdot_nt(wgt_ref[...], h))


def _fox_q(x, mod, norm_g, wqt, wgt, qg_rep, T):
    N, D = x.shape
    tm = qg_rep.shape[1]
    tpb = T // tm
    tile = pl.BlockSpec((tm, D), lambda i: (i, 0))
    ttile = pl.BlockSpec((D, tm), lambda i: (0, i))
    mods = pl.BlockSpec((1, N_MOD, 1, D), lambda i: (i // tpb, 0, 0, 0))
    return pl.pallas_call(
        _fox_q_kernel,
        out_shape=(jax.ShapeDtypeStruct((D, N), BF16), jax.ShapeDtypeStruct((D, N), F32)),
        grid=(N // tm,),
        in_specs=[tile, mods, _const_spec((1, D)), _const_spec(wqt.shape), _const_spec(wgt.shape),
                  _const_spec((D, tm))],
        out_specs=(ttile, ttile),
        compiler_params=_params(("parallel",)),
        name="fox_q",
    )(x, mod, norm_g, wqt, wgt, qg_rep)


def _fox_attn_kernel(tq, tk, sub, qt_ref, k_ref, vt_ref, f_ref, ft_ref, ot_ref, frep_ref):
    p = pl.program_id(1)
    qi = pl.program_id(2)
    T = k_ref.shape[0]
    heads = LANE // HEAD_DIM

    @pl.when(qi == 0)
    def _():
        lane = lax.broadcasted_iota(jnp.int32, (tk, LANE), 1)

        def fill(j, carry):
            start = pl.multiple_of(j * tk, tk)
            blk = f_ref[pl.ds(start, tk), :] * LOG2E
            for hh in range(heads):
                col = jnp.sum(jnp.where(lane == p * heads + hh, blk, 0.0), axis=-1, keepdims=True)
                frep_ref[hh, pl.ds(start, tk), :] = jnp.broadcast_to(col, (tk, LANE))
            return carry
        lax.fori_loop(0, T // tk, fill, 0)

    chan = lax.broadcasted_iota(jnp.int32, (LANE, tq), 0)
    key_in_sub = lax.broadcasted_iota(jnp.int32, (sub, tq), 0)
    query_in_blk = lax.broadcasted_iota(jnp.int32, (sub, tq), 1)
    q = qt_ref[...]
    qz = [jnp.where((chan >= hh * HEAD_DIM) & (chan < (hh + 1) * HEAD_DIM), q, jnp.zeros((), q.dtype))
          for hh in range(heads)]
    f_t = [ft_ref[0, pl.ds(p * heads + hh, 1), :] * LOG2E for hh in range(heads)]
    steps = [(u, hh) for u in range(tk // sub) for hh in range(heads)]

    def tile(j, carry, masked):
        base = j * tk

        def scores(u, hh):
            start = pl.multiple_of(base + u * sub, sub)
            f_s = frep_ref[hh, pl.ds(start, sub), :]
            z = jnp.dot(k_ref[pl.ds(start, sub), :], qz[hh], preferred_element_type=F32)
            z = z - jnp.concatenate([f_s] * (tq // LANE), axis=1)
            if masked:
                z = jnp.where(start + key_in_sub <= qi * tq + query_in_blk, z, -jnp.inf)
            return z

        state = list(carry)
        z_next = scores(*steps[0])
        for idx, (u, hh) in enumerate(steps):
            z = z_next
            if idx + 1 < len(steps):
                z_next = scores(*steps[idx + 1])
            m, l, acc = state[hh]
            m_new = jnp.maximum(m, jnp.max(z, axis=0, keepdims=True) + f_t[hh])
            alpha = jnp.exp2(m - m_new)
            e = jnp.exp2(z - (m_new - f_t[hh]))
            l = alpha * l + jnp.sum(e, axis=0, keepdims=True)
            start = pl.multiple_of(base + u * sub, sub)
            v = vt_ref[hh * HEAD_DIM:(hh + 1) * HEAD_DIM, pl.ds(start, sub)]
            acc = alpha * acc + jnp.dot(v, e.astype(BF16), preferred_element_type=F32)
            state[hh] = (m_new, l, acc)
        return tuple(state)

    init = (jnp.full((1, tq), -jnp.inf, F32), jnp.zeros((1, tq), F32), jnp.zeros((HEAD_DIM, tq), F32))
    n_full = qi * (tq // tk)
    carry = lax.fori_loop(0, n_full, lambda j, c: tile(j, c, False), (init,) * heads)
    for d in range(tq // tk):
        carry = tile(n_full + d, carry, True)
    for hh in range(heads):
        m, l, acc = carry[hh]
        ot_ref[hh * HEAD_DIM:(hh + 1) * HEAD_DIM, :] = acc / l


def _fox_attn(qt, k, vt, f, ft, B, tq=512, tk=512, sub=256):
    D, N = qt.shape
    T = N // B
    nq = T // tq
    qblk = pl.BlockSpec((LANE, tq), lambda b, p, i: (p, b * nq + i))
    return pl.pallas_call(
        functools.partial(_fox_attn_kernel, tq, tk, sub),
        out_shape=jax.ShapeDtypeStruct((D, N), F32),
        grid=(B, D // LANE, nq),
        in_specs=[qblk,
                  pl.BlockSpec((T, LANE), lambda b, p, i: (b, p)),
                  pl.BlockSpec((LANE, T), lambda b, p, i: (p, b)),
                  pl.BlockSpec((T, LANE), lambda b, p, i: (b, 0)),
                  pl.BlockSpec((1, ft.shape[1], tq), lambda b, p, i: (b, 0, i))],
        out_specs=qblk,
        scratch_shapes=[pltpu.VMEM((LANE // HEAD_DIM, T, LANE), F32)],
        compiler_params=_params(("parallel", "parallel", "arbitrary")),
        name="fox_attn",
    )(qt, k, vt, f, ft)


def kernel(x, c, mod_w, mod_b, norm_mix_g, norm_mlp_g, mlp_up, mlp_down, rw_mu, rw_wr, rw_wk, rw_wv, rw_wo, rw_w0, rw_w1, rw_w2, rw_a0, rw_a1, rw_a2, rw_g1, rw_g2, rw_kk, rw_ka, rw_rk, rw_lnw, rw_lnb, rw_v0, rw_v1, rw_v2, kv_norm_g, kv_mod_w, kv_mod_b, kv_w, kv_fb, kv_kg, fx_wqg, fx_qg, fx_wo, final_g):
    B, T, D = x.shape
    depth = mod_w.shape[0]
    n_a = rw_wr.shape[0]
    H = D // HEAD_DIM
    N = B * T
    bf = lambda w: w.astype(BF16)
    row = lambda vct: vct.reshape(1, -1)

    c_pad = jnp.pad(c, ((0, SUBLANE - B), (0, 0)))
    mods = _modulation(c_pad, mod_w, mod_b)[:, :B].reshape(depth, B, N_MOD, 1, D)
    kv_mod = _modulation(c_pad, kv_mod_w[None], kv_mod_b[None])[0, :B].reshape(B, 2, 1, D)

    head_of = jnp.arange(D, dtype=jnp.int32) // HEAD_DIM
    ind = (head_of[:, None] == jnp.arange(LANE, dtype=jnp.int32)[None, :]).astype(F32)
    indt = ind.T

    xf = x.reshape(N, D)
    v_first = None
    k_sh = vt_sh = f_sh = ft_sh = None
    for i in range(depth):
        if i < n_a:
            p = dict(mu=rw_mu[i], wr=bf(rw_wr[i]), wk=bf(rw_wk[i]), wv=bf(rw_wv[i]),
                     w0=row(rw_w0[i]), w1=bf(rw_w1[i]), w2=bf(rw_w2[i]),
                     a0=row(rw_a0[i]), a1=bf(rw_a1[i]), a2=bf(rw_a2[i]),
                     g1=bf(rw_g1[i]), g2=bf(rw_g2[i]), kk=row(rw_kk[i]), ka=row(rw_ka[i]))
            if i > 0:
                p.update(v0=row(rw_v0[i - 1]), v1=bf(rw_v1[i - 1]), v2=bf(rw_v2[i - 1]))
            r, k, v, lw, kk, a, gate = _rwkv_pre(xf, mods[i], row(norm_mix_g[i]), p,
                                                 v_first if i > 0 else None, T)
            if i == 0:
                v_first = v
            m, n, rp, yl = _wkv_prep(r, k, v, lw, kk, a)
            y = _wkv_scan(m, n, rp, yl, r, k, v, row(rw_rk[i]), row(rw_lnw[i]), row(rw_lnb[i]), B)
            gmul, wo = gate, bf(rw_wo[i])
        else:
            j = i - n_a
            if j == 0:
                wf = jnp.pad(kv_w[:, 2 * D:], ((0, 0), (0, LANE - H)))
                fb = jnp.pad(kv_fb, (0, LANE - H)).reshape(1, LANE)
                k_sh, vt_sh, f_sh = _shared_kv(xf, kv_mod, row(kv_norm_g), bf(kv_w[:, :D]),
                                               bf(kv_w[:, D:2 * D].T), bf(wf), fb,
                                               row(jnp.tile(kv_kg, H)), ind, indt, T)
                ft_sh = f_sh[:, :H].reshape(B, T, H).transpose(0, 2, 1)
            qg_rep = jnp.broadcast_to(jnp.tile(fx_qg[j], H)[:, None], (D, TOKEN_TILE))
            qt, gmul = _fox_q(xf, mods[i], row(norm_mix_g[i]), bf(fx_wqg[j][:, :D].T), bf(fx_wqg[j][:, D:].T),
                              qg_rep, T)
            y = _fox_attn(qt, k_sh, vt_sh, f_sh, ft_sh, B)
            wo = bf(fx_wo[j])
        xf = _post_mlp(xf, y, gmul, mods[i], wo, row(norm_mlp_g[i]), bf(mlp_up[i]), bf(mlp_down[i]), T, i >= n_a,
                       final_g=row(final_g) if i == depth - 1 else None)
    return xf.reshape(B, T, D)
```

```python
import functools

import jax
import jax.numpy as jnp
from jax import lax
from jax.experimental import pallas as pl
from jax.experimental.pallas import tpu as pltpu

F32 = jnp.float32
BF16 = jnp.bfloat16

HEAD_DIM = 64
N_MOD = 6
NORM_EPS = 1e-6
GN_EPS = 64e-5
L2_EPS = 1e-12
WKV_CHUNK = 64
LANE = 128
SUBLANE = 8
VMEM_LIMIT = 56 * 1024 * 1024
TOKEN_TILE = 256
LOG2E = 1.4426950408889634
DENOM_ROWS = 16


def _bdot(a, b):
    return jnp.dot(a.astype(BF16), b.astype(BF16), preferred_element_type=F32)


def _bdot_nt(a, b):
    return lax.dot_general(a.astype(BF16), b.astype(BF16), (((1,), (1,)), ((), ())),
                           preferred_element_type=F32)


def _bdot_tn(a, b):
    return lax.dot_general(a.astype(BF16), b.astype(BF16), (((0,), (0,)), ((), ())),
                           preferred_element_type=F32)


def _fdot(a, b):
    return jnp.dot(a, b, preferred_element_type=F32, precision=lax.Precision.HIGHEST)


def _rms_mod(x, g, shift, scale):
    ms = jnp.mean(x * x, axis=-1, keepdims=True)
    y = x * lax.rsqrt(ms + NORM_EPS) * g
    return y * (1.0 + scale) + shift


def _const_spec(shape):
    n = len(shape)
    return pl.BlockSpec(shape, lambda *_: (0,) * n, pipeline_mode=pl.Buffered(1))


def _params(sem):
    return pltpu.CompilerParams(dimension_semantics=sem, vmem_limit_bytes=VMEM_LIMIT)


def _mod_kernel(c_ref, w_ref, b_ref, o_ref):
    c = c_ref[...]
    ca = c * jax.nn.sigmoid(c)
    o_ref[0] = _fdot(ca, w_ref[0]) + b_ref[0]


def _modulation(c_pad, w, b, tn=1024):
    L, D, M = w.shape
    return pl.pallas_call(
        _mod_kernel,
        out_shape=jax.ShapeDtypeStruct((L, SUBLANE, M), F32),
        grid=(L, M // tn),
        in_specs=[pl.BlockSpec((SUBLANE, D), lambda l, j: (0, 0)),
                  pl.BlockSpec((1, D, tn), lambda l, j: (l, 0, j)),
                  pl.BlockSpec((1, 1, tn), lambda l, j: (l, 0, j))],
        out_specs=pl.BlockSpec((1, SUBLANE, tn), lambda l, j: (l, 0, j)),
        compiler_params=_params(("parallel", "parallel")),
        name="modulation",
    )(c_pad, w, b.reshape(L, 1, M))


def _rwkv_pre_kernel(has_vres, tiles_per_batch, *refs):
    if has_vres:
        (x_ref, xp_ref, mod_ref, g_ref, mu_ref, wr_ref, wk_ref, wv_ref, w0_ref, w1_ref, w2_ref,
         a0_ref, a1_ref, a2_ref, g1_ref, g2_ref, kk_ref, ka_ref,
         vf_ref, v0_ref, v1_ref, v2_ref,
         r_out, k_out, v_out, lw_out, kk_out, a_out, g_out) = refs
    else:
        (x_ref, xp_ref, mod_ref, g_ref, mu_ref, wr_ref, wk_ref, wv_ref, w0_ref, w1_ref, w2_ref,
         a0_ref, a1_ref, a2_ref, g1_ref, g2_ref, kk_ref, ka_ref,
         r_out, k_out, v_out, lw_out, kk_out, a_out, g_out) = refs
    i = pl.program_id(0)
    shift, scale = mod_ref[0, 0], mod_ref[0, 1]
    g = g_ref[...]
    h = _rms_mod(x_ref[...], g, shift, scale)
    tm = h.shape[0]
    h_last = _rms_mod(xp_ref[SUBLANE - 1:SUBLANE, :], g, shift, scale)
    h_last = jnp.where(i % tiles_per_batch == 0, 0.0, h_last)
    row = lax.broadcasted_iota(jnp.int32, (tm, 1), 0)
    h_prev = jnp.where(row == 0, h_last, pltpu.roll(h, 1, axis=0))
    xx = h_prev - h
    xr, xw, xk, xv, xa, xg = (h + xx * mu_ref[j:j + 1, :] for j in range(6))

    r = _bdot(xr, wr_ref[...])
    k = _bdot(xk, wk_ref[...])
    v = _bdot(xv, wv_ref[...])
    w_log = -jax.nn.softplus(-(w0_ref[...] + _bdot(jnp.tanh(_bdot(xw, w1_ref[...])), w2_ref[...]))) - 0.5
    if has_vres:
        mix = jax.nn.sigmoid(v0_ref[...] + _bdot(_bdot(xv, v1_ref[...]), v2_ref[...]))
        v = v + (vf_ref[...] - v) * mix
    a = jax.nn.sigmoid(a0_ref[...] + _bdot(_bdot(xa, a1_ref[...]), a2_ref[...]))
    gate = _bdot(jax.nn.sigmoid(_bdot(xg, g1_ref[...])), g2_ref[...])

    r_out[...] = r
    k_out[...] = k * (1.0 + (a - 1.0) * ka_ref[...])
    v_out[...] = v
    lw_out[...] = -jnp.exp(w_log)
    kk_out[...] = k * kk_ref[...]
    a_out[...] = a
    g_out[...] = gate


def _rwkv_pre(x, mod, norm_g, p, v_first, T, tm=TOKEN_TILE):
    N, D = x.shape
    tpb = T // tm
    has_vres = v_first is not None
    tile = pl.BlockSpec((tm, D), lambda i: (i, 0))
    prev = pl.BlockSpec((SUBLANE, D), lambda i: (jnp.maximum(i * (tm // SUBLANE) - 1, 0), 0))
    mods = pl.BlockSpec((1, N_MOD, 1, D), lambda i: (i // tpb, 0, 0, 0))
    vec = _const_spec((1, D))
    args = [x, x, mod, norm_g, p["mu"], p["wr"], p["wk"], p["wv"], p["w0"], p["w1"], p["w2"],
            p["a0"], p["a1"], p["a2"], p["g1"], p["g2"], p["kk"], p["ka"]]
    specs = [tile, prev, mods, vec, _const_spec(p["mu"].shape)]
    specs += [_const_spec(a.shape) for a in args[5:]]
    if has_vres:
        args += [v_first, p["v0"], p["v1"], p["v2"]]
        specs += [tile, vec, _const_spec(p["v1"].shape), _const_spec(p["v2"].shape)]
    out = jax.ShapeDtypeStruct((N, D), F32)
    return pl.pallas_call(
        functools.partial(_rwkv_pre_kernel, has_vres, tpb),
        out_shape=(out,) * 7,
        grid=(N // tm,),
        in_specs=specs,
        out_specs=(tile,) * 7,
        compiler_params=_params(("parallel",)),
        name="rwkv_pre",
    )(*args)


def _blockdiag(w, diag):
    return jnp.where(diag, jnp.concatenate([w, w], axis=0), jnp.zeros((), w.dtype))


def _pair_mm(l, rs, diag):
    rhs = jnp.concatenate([_blockdiag(r.astype(BF16), diag) for r in rs], axis=1)
    return jnp.dot(l.astype(BF16), rhs, preferred_element_type=F32)


def _pair_mm_nt(l, rs, diag):
    rhs = jnp.concatenate([_blockdiag(r.astype(BF16), diag) for r in rs], axis=0)
    return lax.dot_general(l.astype(BF16), rhs, (((1,), (1,)), ((), ())), preferred_element_type=F32)


def _pair_mm_tn(l, rs, first):
    rhs = jnp.concatenate([r.astype(BF16) for r in rs], axis=1)
    full = lax.dot_general(l.astype(BF16), rhs, (((0,), (0,)), ((), ())), preferred_element_type=F32)
    return [jnp.where(first, full[:HEAD_DIM, i * LANE:(i + 1) * LANE], full[HEAD_DIM:, i * LANE:(i + 1) * LANE])
            for i in range(len(rs))]


def _pair_masks(C):
    row = lax.broadcasted_iota(jnp.int32, (C, LANE), 0)
    lane = lax.broadcasted_iota(jnp.int32, (C, LANE), 1)
    col = lane & (HEAD_DIM - 1)
    first = lane < HEAD_DIM
    row2 = lax.broadcasted_iota(jnp.int32, (LANE, LANE), 0)
    lane2 = lax.broadcasted_iota(jnp.int32, (LANE, LANE), 1)
    diag = (row2 < HEAD_DIM) == (lane2 < HEAD_DIM)
    return row, col, first, diag


def _pair_sum(x, first):
    s0 = jnp.sum(jnp.where(first, x, 0.0), axis=-1, keepdims=True)
    s1 = jnp.sum(jnp.where(first, 0.0, x), axis=-1, keepdims=True)
    return jnp.where(first, s0, s1)


def _wkv_prep_kernel(r_ref, k_ref, v_ref, lw_ref, kk_ref, a_ref, m_out, n_out, rp_out, yl_out):
    C = WKV_CHUNK
    chunks = [slice(ci * C, (ci + 1) * C) for ci in range(r_ref.shape[0] // C)]
    row, col, first, diag = _pair_masks(C)
    tri_incl, tri_strict, eye = row >= col, row > col, row == col
    cum_op = (lax.broadcasted_iota(jnp.int32, (C, C), 0) >= lax.broadcasted_iota(jnp.int32, (C, C), 1)).astype(F32)

    def same_block(s):
        return (row ^ col) < s
    levels, s = [], 8
    while s < C:
        levels.append((same_block(s), same_block(2 * s)))
        s *= 2

    at, rt, bt, kt, bh, kh, w_end, vs = [], [], [], [], [], [], [], []
    for rows in chunks:
        lw, kkr = lw_ref[rows, :], kk_ref[rows, :]
        kk = kkr / jnp.maximum(jnp.sqrt(_pair_sum(kkr * kkr, first)), L2_EPS)
        bvec = kk * a_ref[rows, :]
        cw = _fdot(cum_op, lw)
        cw_end = cw[C - 1:C, :]
        w_inv, w_rest = jnp.exp(-cw), jnp.exp(cw_end - cw)
        k = k_ref[rows, :]
        at.append(-kk * jnp.exp(cw - lw))
        rt.append(r_ref[rows, :] * jnp.exp(cw))
        bt.append(bvec * w_inv)
        kt.append(k * w_inv)
        bh.append(bvec * w_rest)
        kh.append(k * w_rest)
        w_end.append(jnp.exp(cw_end))
        vs.append(v_ref[rows, :])
    n = range(len(chunks))

    amat = [_pair_mm_nt(jnp.concatenate([at[i], rt[i]], axis=0), [bt[i], kt[i]], diag) for i in n]
    a_ab = [jnp.where(tri_strict, amat[i][:C, :LANE], 0.0) for i in n]
    a_ak = [jnp.where(tri_strict, amat[i][:C, LANE:], 0.0) for i in n]
    a_rb = [jnp.where(tri_incl, amat[i][C:, :LANE], 0.0) for i in n]
    a_rk = [jnp.where(tri_incl, amat[i][C:, LANE:], 0.0) for i in n]
    akv = [_pair_mm(a_ak[i], [vs[i]], diag) for i in n]

    blk8 = same_block(8)
    x = [jnp.where(blk8, a_ab[i], 0.0) for i in n]
    inv = [jnp.where(eye, 1.0, x[i]) for i in n]
    for _ in range(2):
        x = [_pair_mm(x[i], [x[i]], diag) for i in n]
        inv = [inv[i] + _pair_mm(x[i], [inv[i]], diag) for i in n]
    for inner, outer in levels:
        off = [jnp.where(outer, jnp.where(inner, 0.0, a_ab[i]), 0.0) for i in n]
        t = [_pair_mm(off[i], [inv[i]], diag) for i in n]
        inv = [inv[i] + _pair_mm(inv[i], [t[i]], diag) for i in n]

    sol = [_pair_mm(inv[i], [at[i], akv[i]], diag) for i in n]
    ap = [sol[i][:, :LANE] for i in n]
    ul = [sol[i][:, LANE:] for i in n]
    for i in n:
        rows = chunks[i]
        rb = _pair_mm(a_rb[i], [ap[i], ul[i]], diag)
        rp_out[rows, :] = rt[i] + rb[:, :LANE]
        yl_out[rows, :] = rb[:, LANE:] + _pair_mm(a_rk[i], [vs[i]], diag)
        bh_ap, bh_ul = _pair_mm_tn(bh[i], [ap[i], ul[i]], first)
        m_out[rows, :] = jnp.where(eye, w_end[i], 0.0) + bh_ap
        n_out[rows, :] = bh_ul + _pair_mm_tn(kh[i], [vs[i]], first)[0]


def _wkv_prep(r, k, v, lw, kk, a, tb=512):
    N, D = r.shape
    assert WKV_CHUNK == HEAD_DIM and 2 * HEAD_DIM == LANE
    blk = pl.BlockSpec((tb, LANE), lambda i, p: (i, p))
    out = jax.ShapeDtypeStruct((N, D), F32)
    return pl.pallas_call(
        _wkv_prep_kernel,
        out_shape=(out,) * 4,
        grid=(N // tb, D // LANE),
        in_specs=[blk] * 6,
        out_specs=(blk,) * 4,
        compiler_params=_params(("parallel", "parallel")),
        name="wkv_prep",
    )(r, k, v, lw, kk, a)


def _wkv_scan_kernel(m_ref, n_ref, rp_ref, yl_ref, r_ref, k_ref, v_ref, rk_ref, lnw_ref, lnb_ref,
                     y_out, s_ref):
    C = WKV_CHUNK
    B, tb, D = m_ref.shape
    _, _, first, diag = _pair_masks(C)

    @pl.when(pl.program_id(0) == 0)
    def _():
        s_ref[...] = jnp.zeros_like(s_ref)

    pairs = [(b, slice(p * LANE, (p + 1) * LANE)) for b in range(B) for p in range(D // LANE)]
    for ci in range(tb // C):
        rows = slice(ci * C, (ci + 1) * C)
        sdiag = [_blockdiag(s_ref[b, :, lanes], diag) for b, lanes in pairs]
        ys = [_fdot(rp_ref[b, rows, lanes], sd) + yl_ref[b, rows, lanes] for (b, lanes), sd in zip(pairs, sdiag)]
        for (b, lanes), sd in zip(pairs, sdiag):
            s_ref[b, :, lanes] = _fdot(m_ref[b, rows, lanes], sd) + n_ref[b, rows, lanes]
        for (b, lanes), y in zip(pairs, ys):
            yc = y - _pair_sum(y, first) * (1.0 / HEAD_DIM)
            var = _pair_sum(yc * yc, first) * (1.0 / HEAD_DIM)
            yn = yc * lax.rsqrt(var + GN_EPS) * lnw_ref[:, lanes] + lnb_ref[:, lanes]
            rk = r_ref[b, rows, lanes] * k_ref[b, rows, lanes] * rk_ref[:, lanes]
            y_out[b, rows, lanes] = yn + _pair_sum(rk, first) * v_ref[b, rows, lanes]


def _wkv_scan(m, n, rp, yl, r, k, v, rk, lnw, lnb, B, tb=128):
    N, D = m.shape
    T = N // B
    blk = pl.BlockSpec((B, tb, D), lambda t: (0, t, 0))
    vec = pl.BlockSpec((1, D), lambda t: (0, 0))
    args = [a.reshape(B, T, D) for a in (m, n, rp, yl, r, k, v)]
    return pl.pallas_call(
        _wkv_scan_kernel,
        out_shape=jax.ShapeDtypeStruct((B, T, D), F32),
        grid=(T // tb,),
        in_specs=[blk] * 7 + [vec] * 3,
        out_specs=blk,
        scratch_shapes=[pltpu.VMEM((B, HEAD_DIM, D), F32)],
        compiler_params=_params(("arbitrary",)),
        name="wkv_scan",
    )(*args, rk, lnw, lnb).reshape(N, D)


def _post_mlp_kernel(final, channel_major, *refs):
    if final:
        (x_ref, y_ref, gm_ref, mod_ref, wo_ref, g_ref, up_ref, down_ref, fg_ref, o_ref) = refs
    else:
        (x_ref, y_ref, gm_ref, mod_ref, wo_ref, g_ref, up_ref, down_ref, o_ref) = refs
    gt1, sh2, sc2, gt2 = mod_ref[0, 2], mod_ref[0, 3], mod_ref[0, 4], mod_ref[0, 5]
    gated = y_ref[...] * gm_ref[...]
    if channel_major:
        mix = _bdot_tn(gated, wo_ref[...])
    else:
        mix = _bdot(gated, wo_ref[...])
    x = x_ref[...] + (1.0 + gt1) * mix
    h = _rms_mod(x, g_ref[...], sh2, sc2)
    u = jnp.maximum(_bdot(h, up_ref[...]), 0.0)
    x = x + (1.0 + gt2) * _bdot(u * u, down_ref[...])
    if final:
        ms = jnp.mean(x * x, axis=-1, keepdims=True)
        x = x * lax.rsqrt(ms + NORM_EPS) * fg_ref[...]
    o_ref[...] = x


def _post_mlp(x, y, gmul, mod, wo, norm_g, w_up, w_down, T, channel_major, final_g=None, tm=TOKEN_TILE):
    N, D = x.shape
    tpb = T // tm
    final = final_g is not None
    tile = pl.BlockSpec((tm, D), lambda i: (i, 0))
    mixer = pl.BlockSpec((D, tm), lambda i: (0, i)) if channel_major else tile
    mods = pl.BlockSpec((1, N_MOD, 1, D), lambda i: (i // tpb, 0, 0, 0))
    args = [x, y, gmul, mod, wo, norm_g, w_up, w_down]
    specs = [tile, mixer, mixer, mods, _const_spec(wo.shape), _const_spec((1, D)),
             _const_spec(w_up.shape), _const_spec(w_down.shape)]
    if final:
        args.append(final_g)
        specs.append(_const_spec((1, D)))
    return pl.pallas_call(
        functools.partial(_post_mlp_kernel, final, channel_major),
        out_shape=jax.ShapeDtypeStruct((N, D), F32),
        grid=(N // tm,),
        in_specs=specs,
        out_specs=tile,
        compiler_params=_params(("parallel",)),
        name="post_mlp",
    )(*args)


def _shared_kv_kernel(tiles_per_batch, x_ref, mod_ref, g_ref, wk_ref, wvt_ref, wf_ref, fb_ref, kg_ref,
                      ind_ref, indt_ref, k_out, vt_out, f_out, carry_ref):
    i = pl.program_id(0)

    @pl.when(i % tiles_per_batch == 0)
    def _():
        carry_ref[...] = jnp.zeros_like(carry_ref)

    h = _rms_mod(x_ref[...], g_ref[...], mod_ref[0, 0], mod_ref[0, 1])
    tm = h.shape[0]
    k = _bdot(h, wk_ref[...])
    ss = _fdot(k * k, ind_ref[...])
    ms = _fdot(ss, indt_ref[...]) * (1.0 / HEAD_DIM)
    k_out[...] = (k * lax.rsqrt(ms + NORM_EPS) * kg_ref[...]).astype(BF16)
    vt_out[...] = _bdot_nt(wvt_ref[...], h).astype(BF16)
    log_f = jax.nn.log_sigmoid(_bdot(h, wf_ref[...]) + fb_ref[...])
    row = lax.broadcasted_iota(jnp.int32, (tm, tm), 0)
    col = lax.broadcasted_iota(jnp.int32, (tm, tm), 1)
    cum = _fdot((row >= col).astype(F32), log_f) + carry_ref[...]
    f_out[...] = cum
    carry_ref[...] = cum[tm - 1:tm, :]


def _shared_kv(x, mod, norm_g, wk, wvt, wf, fb, kg, ind, indt, T, tm=TOKEN_TILE):
    N, D = x.shape
    tpb = T // tm
    tile = pl.BlockSpec((tm, D), lambda i: (i, 0))
    mods = pl.BlockSpec((1, 2, 1, D), lambda i: (i // tpb, 0, 0, 0))
    return pl.pallas_call(
        functools.partial(_shared_kv_kernel, tpb),
        out_shape=(jax.ShapeDtypeStruct((N, D), BF16), jax.ShapeDtypeStruct((D, N), BF16),
                   jax.ShapeDtypeStruct((N, LANE), F32)),
        grid=(N // tm,),
        in_specs=[tile, mods, _const_spec((1, D)), _const_spec(wk.shape), _const_spec(wvt.shape),
                  _const_spec(wf.shape), _const_spec((1, LANE)), _const_spec((1, D)),
                  _const_spec(ind.shape), _const_spec(indt.shape)],
        out_specs=(tile, pl.BlockSpec((D, tm), lambda i: (0, i)), pl.BlockSpec((tm, LANE), lambda i: (i, 0))),
        scratch_shapes=[pltpu.VMEM((1, LANE), F32)],
        compiler_params=_params(("arbitrary",)),
        name="shared_kv",
    )(x, mod, norm_g, wk, wvt, wf, fb, kg, ind, indt)


def _fox_q_kernel(x_ref, mod_ref, g_ref, wqt_ref, wgt_ref, qg_ref, qt_out, sgt_out):
    h = _rms_mod(x_ref[...], g_ref[...], mod_ref[0, 0], mod_ref[0, 1])
    tm = h.shape[0]
    qt = _bdot_nt(wqt_ref[...], h)
    q3 = qt.reshape(qt.shape[0] // HEAD_DIM, HEAD_DIM, tm)
    q3 = q3 * lax.rsqrt(jnp.mean(q3 * q3, axis=1, keepdims=True) + NORM_EPS)
    qt_out[...] = (q3.reshape(qt.shape) * qg_ref[...] * (HEAD_DIM ** -0.5 * LOG2E)).astype(BF16)
    sgt_out[...] = jax.nn.sigmoid(_bdot_nt(wgt_ref[...], h))


def _fox_q(x, mod, norm_g, wqt, wgt, qg_rep, T):
    N, D = x.shape
    tm = qg_rep.shape[1]
    tpb = T // tm
    tile = pl.BlockSpec((tm, D), lambda i: (i, 0))
    ttile = pl.BlockSpec((D, tm), lambda i: (0, i))
    mods = pl.BlockSpec((1, N_MOD, 1, D), lambda i: (i // tpb, 0, 0, 0))
    return pl.pallas_call(
        _fox_q_kernel,
        out_shape=(jax.ShapeDtypeStruct((D, N), BF16), jax.ShapeDtypeStruct((D, N), F32)),
        grid=(N // tm,),
        in_specs=[tile, mods, _const_spec((1, D)), _const_spec(wqt.shape), _const_spec(wgt.shape),
                  _const_spec((D, tm))],
        out_specs=(ttile, ttile),
        compiler_params=_params(("parallel",)),
        name="fox_q",
    )(x, mod, norm_g, wqt, wgt, qg_rep)


def _fox_attn_kernel(tq, tk, sub, ahead, qt_ref, k_ref, vt_ref, f_ref, ft_ref, ot_ref, frep_ref, z_ref):
    p = pl.program_id(1)
    qi = pl.program_id(2)
    T = k_ref.shape[0]
    heads = LANE // HEAD_DIM

    @pl.when(qi == 0)
    def _():
        lane = lax.broadcasted_iota(jnp.int32, (tk, LANE), 1)

        def fill(j, carry):
            start = pl.multiple_of(j * tk, tk)
            blk = f_ref[pl.ds(start, tk), :] * LOG2E
            for hh in range(heads):
                col = jnp.sum(jnp.where(lane == p * heads + hh, blk, 0.0), axis=-1, keepdims=True)
                frep_ref[hh, pl.ds(start, tk), :] = jnp.broadcast_to(col, (tk, LANE))
            return carry
        lax.fori_loop(0, T // tk, fill, 0)

    chan = lax.broadcasted_iota(jnp.int32, (LANE, tq), 0)
    key_in_sub = lax.broadcasted_iota(jnp.int32, (sub, tq), 0)
    query_in_blk = lax.broadcasted_iota(jnp.int32, (sub, tq), 1)
    q = qt_ref[...]
    qz = [jnp.where((chan >= hh * HEAD_DIM) & (chan < (hh + 1) * HEAD_DIM), q, jnp.zeros((), q.dtype))
          for hh in range(heads)]
    f_t = [ft_ref[0, pl.ds(p * heads + hh, 1), :] * LOG2E for hh in range(heads)]
    steps = [(u, hh) for u in range(tk // sub) for hh in range(heads)]

    n_steps, n_slots = len(steps), ahead + 1
    assert n_steps % n_slots == 0 and ahead <= n_steps

    def scores(j, idx, masked):
        u, hh = steps[idx]
        start = pl.multiple_of(j * tk + u * sub, sub)
        f_s = frep_ref[hh, pl.ds(start, sub), :]
        z = jnp.dot(k_ref[pl.ds(start, sub), :], qz[hh], preferred_element_type=F32)
        z = z - jnp.concatenate([f_s] * (tq // LANE), axis=1)
        if masked:
            z = jnp.where(start + key_in_sub <= qi * tq + query_in_blk, z, -jnp.inf)
        z_ref[idx % n_slots] = z

    def tile(j, carry, masked, next_masked):
        state = list(carry)
        for idx, (u, hh) in enumerate(steps):
            if idx + ahead < n_steps:
                scores(j, idx + ahead, masked)
            elif next_masked is not None:
                scores(j + 1, idx + ahead - n_steps, next_masked)
            z = z_ref[idx % n_slots]
            m, acc = state[hh]
            m_new = jnp.maximum(m, jnp.max(z, axis=0, keepdims=True) + f_t[hh])
            alpha = jnp.exp2(m - m_new)
            e = jnp.exp2((z - (m_new - f_t[hh])).astype(BF16))
            start = pl.multiple_of(j * tk + u * sub, sub)
            v = vt_ref[hh * HEAD_DIM:(hh + 1) * HEAD_DIM, pl.ds(start, sub)]
            v1 = jnp.concatenate([v, ones_rows], axis=0)
            acc = alpha * acc + jnp.dot(v1, e, preferred_element_type=F32)
            state[hh] = (m_new, acc)
        return tuple(state)

    ones_rows = jnp.ones((DENOM_ROWS, sub), BF16)
    init = (jnp.full((1, tq), -jnp.inf, F32), jnp.zeros((HEAD_DIM + DENOM_ROWS, tq), F32))
    n_full = qi * (tq // tk)
    n_diag = tq // tk
    for i in range(ahead):
        scores(0, i, True)
    carry = lax.fori_loop(0, n_full - 1, lambda j, c: tile(j, c, False, False), (init,) * heads)
    carry = lax.fori_loop(jnp.maximum(n_full - 1, 0), n_full, lambda j, c: tile(j, c, False, True), carry)
    for d in range(n_diag):
        carry = tile(n_full + d, carry, True, True if d + 1 < n_diag else None)
    for hh in range(heads):
        m, acc = carry[hh]
        ot_ref[hh * HEAD_DIM:(hh + 1) * HEAD_DIM, :] = acc[:HEAD_DIM] / acc[HEAD_DIM:HEAD_DIM + 1]


def _fox_attn(qt, k, vt, f, ft, B, tq=512, tk=512, sub=256, ahead=3):
    D, N = qt.shape
    T = N // B
    nq = T // tq
    qblk = pl.BlockSpec((LANE, tq), lambda b, p, i: (p, b * nq + i))
    return pl.pallas_call(
        functools.partial(_fox_attn_kernel, tq, tk, sub, ahead),
        out_shape=jax.ShapeDtypeStruct((D, N), F32),
        grid=(B, D // LANE, nq),
        in_specs=[qblk,
                  pl.BlockSpec((T, LANE), lambda b, p, i: (b, p)),
                  pl.BlockSpec((LANE, T), lambda b, p, i: (p, b)),
                  pl.BlockSpec((T, LANE), lambda b, p, i: (b, 0)),
                  pl.BlockSpec((1, ft.shape[1], tq), lambda b, p, i: (b, 0, i))],
        out_specs=qblk,
        scratch_shapes=[pltpu.VMEM((LANE // HEAD_DIM, T, LANE), F32),
                        pltpu.VMEM((ahead + 1, sub, tq), F32)],
        compiler_params=_params(("parallel", "parallel", "arbitrary")),
        name="fox_attn",
    )(qt, k, vt, f, ft)


def kernel(x, c, mod_w, mod_b, norm_mix_g, norm_mlp_g, mlp_up, mlp_down, rw_mu, rw_wr, rw_wk, rw_wv, rw_wo, rw_w0, rw_w1, rw_w2, rw_a0, rw_a1, rw_a2, rw_g1, rw_g2, rw_kk, rw_ka, rw_rk, rw_lnw, rw_lnb, rw_v0, rw_v1, rw_v2, kv_norm_g, kv_mod_w, kv_mod_b, kv_w, kv_fb, kv_kg, fx_wqg, fx_qg, fx_wo, final_g):
    B, T, D = x.shape
    depth = mod_w.shape[0]
    n_a = rw_wr.shape[0]
    H = D // HEAD_DIM
    N = B * T
    bf = lambda w: w.astype(BF16)
    row = lambda vct: vct.reshape(1, -1)

    c_pad = jnp.pad(c, ((0, SUBLANE - B), (0, 0)))
    mods = _modulation(c_pad, mod_w, mod_b)[:, :B].reshape(depth, B, N_MOD, 1, D)
    kv_mod = _modulation(c_pad, kv_mod_w[None], kv_mod_b[None])[0, :B].reshape(B, 2, 1, D)

    head_of = jnp.arange(D, dtype=jnp.int32) // HEAD_DIM
    ind = (head_of[:, None] == jnp.arange(LANE, dtype=jnp.int32)[None, :]).astype(F32)
    indt = ind.T

    xf = x.reshape(N, D)
    v_first = None
    k_sh = vt_sh = f_sh = ft_sh = None
    for i in range(depth):
        if i < n_a:
            p = dict(mu=rw_mu[i], wr=bf(rw_wr[i]), wk=bf(rw_wk[i]), wv=bf(rw_wv[i]),
                     w0=row(rw_w0[i]), w1=bf(rw_w1[i]), w2=bf(rw_w2[i]),
                     a0=row(rw_a0[i]), a1=bf(rw_a1[i]), a2=bf(rw_a2[i]),
                     g1=bf(rw_g1[i]), g2=bf(rw_g2[i]), kk=row(rw_kk[i]), ka=row(rw_ka[i]))
            if i > 0:
                p.update(v0=row(rw_v0[i - 1]), v1=bf(rw_v1[i - 1]), v2=bf(rw_v2[i - 1]))
            r, k, v, lw, kk, a, gate = _rwkv_pre(xf, mods[i], row(norm_mix_g[i]), p,
                                                 v_first if i > 0 else None, T)
            if i == 0:
                v_first = v
            m, n, rp, yl = _wkv_prep(r, k, v, lw, kk, a)
            y = _wkv_scan(m, n, rp, yl, r, k, v, row(rw_rk[i]), row(rw_lnw[i]), row(rw_lnb[i]), B)
            gmul, wo = gate, bf(rw_wo[i])
        else:
            j = i - n_a
            if j == 0:
                wf = jnp.pad(kv_w[:, 2 * D:], ((0, 0), (0, LANE - H)))
                fb = jnp.pad(kv_fb, (0, LANE - H)).reshape(1, LANE)
                k_sh, vt_sh, f_sh = _shared_kv(xf, kv_mod, row(kv_norm_g), bf(kv_w[:, :D]),
                                               bf(kv_w[:, D:2 * D].T), bf(wf), fb,
                                               row(jnp.tile(kv_kg, H)), ind, indt, T)
                ft_sh = f_sh[:, :H].reshape(B, T, H).transpose(0, 2, 1)
            qg_rep = jnp.broadcast_to(jnp.tile(fx_qg[j], H)[:, None], (D, TOKEN_TILE))
            qt, gmul = _fox_q(xf, mods[i], row(norm_mix_g[i]), bf(fx_wqg[j][:, :D].T), bf(fx_wqg[j][:, D:].T),
                              qg_rep, T)
            y = _fox_attn(qt, k_sh, vt_sh, f_sh, ft_sh, B)
            wo = bf(fx_wo[j])
        xf = _post_mlp(xf, y, gmul, mods[i], wo, row(norm_mlp_g[i]), bf(mlp_up[i]), bf(mlp_down[i]), T, i >= n_a,
                       final_g=row(final_g) if i == depth - 1 else None)
    return xf.reshape(B, T, D)
```

```python
import functools

import jax
import jax.numpy as jnp
from jax import lax
from jax.experimental import pallas as pl
from jax.experimental.pallas import tpu as pltpu

F32 = jnp.float32
BF16 = jnp.bfloat16

HEAD_DIM = 64
N_MOD = 6
NORM_EPS = 1e-6
GN_EPS = 64e-5
L2_EPS = 1e-12
WKV_CHUNK = 64
LANE = 128
SUBLANE = 8
VMEM_LIMIT = 56 * 1024 * 1024
TOKEN_TILE = 256
LOG2E = 1.4426950408889634
DENOM_ROWS = 16
F_PIECES = 3


def _bdot(a, b):
    return jnp.dot(a.astype(BF16), b.astype(BF16), preferred_element_type=F32)


def _bdot_nt(a, b):
    return lax.dot_general(a.astype(BF16), b.astype(BF16), (((1,), (1,)), ((), ())),
                           preferred_element_type=F32)


def _bdot_tn(a, b):
    return lax.dot_general(a.astype(BF16), b.astype(BF16), (((0,), (0,)), ((), ())),
                           preferred_element_type=F32)


def _fdot(a, b):
    return jnp.dot(a, b, preferred_element_type=F32, precision=lax.Precision.HIGHEST)


def _split2(x):
    hi = x.astype(BF16)
    return hi, (x - hi.astype(F32)).astype(BF16)


def _split3(x):
    hi = x.astype(BF16)
    rest = x - hi.astype(F32)
    mid = rest.astype(BF16)
    lo = (rest - mid.astype(F32)).astype(BF16)
    return hi, mid, lo


def _dot_sel_lhs(sel, x):
    n = x.shape[1]
    y = jnp.dot(sel.astype(BF16), jnp.concatenate(_split3(x), axis=1), preferred_element_type=F32)
    return y[:, :n] + y[:, n:2 * n] + y[:, 2 * n:]


def _dot_sel_rhs(x, sel):
    m = x.shape[0]
    y = jnp.dot(jnp.concatenate(_split3(x), axis=0), sel.astype(BF16), preferred_element_type=F32)
    return y[:m] + y[m:2 * m] + y[2 * m:]


def _rms_mod(x, g, shift, scale):
    ms = jnp.mean(x * x, axis=-1, keepdims=True)
    y = x * lax.rsqrt(ms + NORM_EPS) * g
    return y * (1.0 + scale) + shift


def _const_spec(shape):
    n = len(shape)
    return pl.BlockSpec(shape, lambda *_: (0,) * n, pipeline_mode=pl.Buffered(1))


def _params(sem):
    return pltpu.CompilerParams(dimension_semantics=sem, vmem_limit_bytes=VMEM_LIMIT)


def _mod_kernel(c_ref, w_ref, b_ref, o_ref):
    c = c_ref[...]
    ca = c * jax.nn.sigmoid(c)
    o_ref[0] = _fdot(ca, w_ref[0]) + b_ref[0]


def _modulation(c_pad, w, b, tn=1024):
    L, D, M = w.shape
    return pl.pallas_call(
        _mod_kernel,
        out_shape=jax.ShapeDtypeStruct((L, SUBLANE, M), F32),
        grid=(L, M // tn),
        in_specs=[pl.BlockSpec((SUBLANE, D), lambda l, j: (0, 0)),
                  pl.BlockSpec((1, D, tn), lambda l, j: (l, 0, j)),
                  pl.BlockSpec((1, 1, tn), lambda l, j: (l, 0, j))],
        out_specs=pl.BlockSpec((1, SUBLANE, tn), lambda l, j: (l, 0, j)),
        compiler_params=_params(("parallel", "parallel")),
        name="modulation",
    )(c_pad, w, b.reshape(L, 1, M))


def _rwkv_pre_kernel(has_vres, tiles_per_batch, *refs):
    if has_vres:
        (x_ref, xp_ref, mod_ref, g_ref, mu_ref, wr_ref, wk_ref, wv_ref, w0_ref, w1_ref, w2_ref,
         a0_ref, a1_ref, a2_ref, g1_ref, g2_ref, kk_ref, ka_ref,
         vf_ref, v0_ref, v1_ref, v2_ref,
         r_out, k_out, v_out, lw_out, kk_out, a_out, g_out) = refs
    else:
        (x_ref, xp_ref, mod_ref, g_ref, mu_ref, wr_ref, wk_ref, wv_ref, w0_ref, w1_ref, w2_ref,
         a0_ref, a1_ref, a2_ref, g1_ref, g2_ref, kk_ref, ka_ref,
         r_out, k_out, v_out, lw_out, kk_out, a_out, g_out) = refs
    i = pl.program_id(0)
    shift, scale = mod_ref[0, 0], mod_ref[0, 1]
    g = g_ref[...]
    h = _rms_mod(x_ref[...], g, shift, scale)
    tm = h.shape[0]
    h_last = _rms_mod(xp_ref[SUBLANE - 1:SUBLANE, :], g, shift, scale)
    h_last = jnp.where(i % tiles_per_batch == 0, 0.0, h_last)
    row = lax.broadcasted_iota(jnp.int32, (tm, 1), 0)
    h_prev = jnp.where(row == 0, h_last, pltpu.roll(h, 1, axis=0))
    xx = h_prev - h
    xr, xw, xk, xv, xa, xg = (h + xx * mu_ref[j:j + 1, :] for j in range(6))

    r = _bdot(xr, wr_ref[...])
    k = _bdot(xk, wk_ref[...])
    v = _bdot(xv, wv_ref[...])
    w_log = -jax.nn.softplus(-(w0_ref[...] + _bdot(jnp.tanh(_bdot(xw, w1_ref[...])), w2_ref[...]))) - 0.5
    if has_vres:
        mix = jax.nn.sigmoid(v0_ref[...] + _bdot(_bdot(xv, v1_ref[...]), v2_ref[...]))
        v = v + (vf_ref[...] - v) * mix
    a = jax.nn.sigmoid(a0_ref[...] + _bdot(_bdot(xa, a1_ref[...]), a2_ref[...]))
    gate = _bdot(jax.nn.sigmoid(_bdot(xg, g1_ref[...])), g2_ref[...])

    r_out[...] = r
    k_out[...] = k * (1.0 + (a - 1.0) * ka_ref[...])
    v_out[...] = v
    lw_out[...] = -jnp.exp(w_log)
    kk_out[...] = k * kk_ref[...]
    a_out[...] = a
    g_out[...] = gate


def _rwkv_pre(x, mod, norm_g, p, v_first, T, tm=TOKEN_TILE):
    N, D = x.shape
    tpb = T // tm
    has_vres = v_first is not None
    tile = pl.BlockSpec((tm, D), lambda i: (i, 0))
    prev = pl.BlockSpec((SUBLANE, D), lambda i: (jnp.maximum(i * (tm // SUBLANE) - 1, 0), 0))
    mods = pl.BlockSpec((1, N_MOD, 1, D), lambda i: (i // tpb, 0, 0, 0))
    vec = _const_spec((1, D))
    args = [x, x, mod, norm_g, p["mu"], p["wr"], p["wk"], p["wv"], p["w0"], p["w1"], p["w2"],
            p["a0"], p["a1"], p["a2"], p["g1"], p["g2"], p["kk"], p["ka"]]
    specs = [tile, prev, mods, vec, _const_spec(p["mu"].shape)]
    specs += [_const_spec(a.shape) for a in args[5:]]
    if has_vres:
        args += [v_first, p["v0"], p["v1"], p["v2"]]
        specs += [tile, vec, _const_spec(p["v1"].shape), _const_spec(p["v2"].shape)]
    out = jax.ShapeDtypeStruct((N, D), F32)
    return pl.pallas_call(
        functools.partial(_rwkv_pre_kernel, has_vres, tpb),
        out_shape=(out,) * 7,
        grid=(N // tm,),
        in_specs=specs,
        out_specs=(tile,) * 7,
        compiler_params=_params(("parallel",)),
        name="rwkv_pre",
    )(*args)


def _blockdiag(w, diag):
    return jnp.where(diag, jnp.concatenate([w, w], axis=0), jnp.zeros((), w.dtype))


def _pair_mm(l, rs, diag):
    rhs = jnp.concatenate([_blockdiag(r.astype(BF16), diag) for r in rs], axis=1)
    return jnp.dot(l.astype(BF16), rhs, preferred_element_type=F32)


def _pair_mm_nt(l, rs, diag):
    rhs = jnp.concatenate([_blockdiag(r.astype(BF16), diag) for r in rs], axis=0)
    return lax.dot_general(l.astype(BF16), rhs, (((1,), (1,)), ((), ())), preferred_element_type=F32)


def _pair_mm_tn(l, rs, first):
    rhs = jnp.concatenate([r.astype(BF16) for r in rs], axis=1)
    full = lax.dot_general(l.astype(BF16), rhs, (((0,), (0,)), ((), ())), preferred_element_type=F32)
    return [jnp.where(first, full[:HEAD_DIM, i * LANE:(i + 1) * LANE], full[HEAD_DIM:, i * LANE:(i + 1) * LANE])
            for i in range(len(rs))]


def _pair_masks(C):
    row = lax.broadcasted_iota(jnp.int32, (C, LANE), 0)
    lane = lax.broadcasted_iota(jnp.int32, (C, LANE), 1)
    col = lane & (HEAD_DIM - 1)
    first = lane < HEAD_DIM
    row2 = lax.broadcasted_iota(jnp.int32, (LANE, LANE), 0)
    lane2 = lax.broadcasted_iota(jnp.int32, (LANE, LANE), 1)
    diag = (row2 < HEAD_DIM) == (lane2 < HEAD_DIM)
    return row, col, first, diag


def _pair_sum(x, first):
    s0 = jnp.sum(jnp.where(first, x, 0.0), axis=-1, keepdims=True)
    s1 = jnp.sum(jnp.where(first, 0.0, x), axis=-1, keepdims=True)
    return jnp.where(first, s0, s1)


def _wkv_prep_kernel(r_ref, k_ref, v_ref, lw_ref, kk_ref, a_ref, m_out, n_out, rp_out, yl_out):
    C = WKV_CHUNK
    chunks = [slice(ci * C, (ci + 1) * C) for ci in range(r_ref.shape[0] // C)]
    row, col, first, diag = _pair_masks(C)
    tri_incl, tri_strict, eye = row >= col, row > col, row == col
    cum_op = (lax.broadcasted_iota(jnp.int32, (C, C), 0) >= lax.broadcasted_iota(jnp.int32, (C, C), 1)).astype(F32)

    def same_block(s):
        return (row ^ col) < s
    levels, s = [], 8
    while s < C:
        levels.append((same_block(s), same_block(2 * s)))
        s *= 2

    at, rt, bt, kt, bh, kh, w_end, vs = [], [], [], [], [], [], [], []
    for rows in chunks:
        lw, kkr = lw_ref[rows, :], kk_ref[rows, :]
        kk = kkr / jnp.maximum(jnp.sqrt(_pair_sum(kkr * kkr, first)), L2_EPS)
        bvec = kk * a_ref[rows, :]
        cw = _dot_sel_lhs(cum_op, lw)
        cw_end = cw[C - 1:C, :]
        w_inv, w_rest = jnp.exp(-cw), jnp.exp(cw_end - cw)
        k = k_ref[rows, :]
        at.append(-kk * jnp.exp(cw - lw))
        rt.append(r_ref[rows, :] * jnp.exp(cw))
        bt.append(bvec * w_inv)
        kt.append(k * w_inv)
        bh.append(bvec * w_rest)
        kh.append(k * w_rest)
        w_end.append(jnp.exp(cw_end))
        vs.append(v_ref[rows, :])
    n = range(len(chunks))

    amat = [_pair_mm_nt(jnp.concatenate([at[i], rt[i]], axis=0), [bt[i], kt[i]], diag) for i in n]
    a_ab = [jnp.where(tri_strict, amat[i][:C, :LANE], 0.0) for i in n]
    a_ak = [jnp.where(tri_strict, amat[i][:C, LANE:], 0.0) for i in n]
    a_rb = [jnp.where(tri_incl, amat[i][C:, :LANE], 0.0) for i in n]
    a_rk = [jnp.where(tri_incl, amat[i][C:, LANE:], 0.0) for i in n]
    akv = [_pair_mm(a_ak[i], [vs[i]], diag) for i in n]

    blk8 = same_block(8)
    x = [jnp.where(blk8, a_ab[i], 0.0) for i in n]
    inv = [jnp.where(eye, 1.0, x[i]) for i in n]
    for _ in range(2):
        x = [_pair_mm(x[i], [x[i]], diag) for i in n]
        inv = [inv[i] + _pair_mm(x[i], [inv[i]], diag) for i in n]
    for inner, outer in levels:
        off = [jnp.where(outer, jnp.where(inner, 0.0, a_ab[i]), 0.0) for i in n]
        t = [_pair_mm(off[i], [inv[i]], diag) for i in n]
        inv = [inv[i] + _pair_mm(inv[i], [t[i]], diag) for i in n]

    sol = [_pair_mm(inv[i], [at[i], akv[i]], diag) for i in n]
    ap = [sol[i][:, :LANE] for i in n]
    ul = [sol[i][:, LANE:] for i in n]
    for i in n:
        rows = chunks[i]
        rb = _pair_mm(a_rb[i], [ap[i], ul[i]], diag)
        rp_out[rows, :] = rt[i] + rb[:, :LANE]
        yl_out[rows, :] = rb[:, LANE:] + _pair_mm(a_rk[i], [vs[i]], diag)
        bh_ap, bh_ul = _pair_mm_tn(bh[i], [ap[i], ul[i]], first)
        m_out[rows, :] = jnp.where(eye, w_end[i], 0.0) + bh_ap
        n_out[rows, :] = bh_ul + _pair_mm_tn(kh[i], [vs[i]], first)[0]


def _wkv_prep(r, k, v, lw, kk, a, tb=1024):
    N, D = r.shape
    assert WKV_CHUNK == HEAD_DIM and 2 * HEAD_DIM == LANE
    blk = pl.BlockSpec((tb, LANE), lambda i, p: (i, p))
    out = jax.ShapeDtypeStruct((N, D), F32)
    return pl.pallas_call(
        _wkv_prep_kernel,
        out_shape=(out,) * 4,
        grid=(N // tb, D // LANE),
        in_specs=[blk] * 6,
        out_specs=(blk,) * 4,
        compiler_params=_params(("parallel", "parallel")),
        name="wkv_prep",
    )(r, k, v, lw, kk, a)


def _wkv_scan_kernel(m_ref, n_ref, rp_ref, yl_ref, r_ref, k_ref, v_ref, rk_ref, lnw_ref, lnb_ref,
                     y_out, s_ref):
    C = WKV_CHUNK
    B, tb, D = m_ref.shape
    _, _, first, diag = _pair_masks(C)

    @pl.when(pl.program_id(0) == 0)
    def _():
        s_ref[...] = jnp.zeros_like(s_ref)

    pairs = [(b, slice(p * LANE, (p + 1) * LANE)) for b in range(B) for p in range(D // LANE)]
    for ci in range(tb // C):
        rows = slice(ci * C, (ci + 1) * C)
        prods = []
        for b, lanes in pairs:
            s_hi, s_lo = _split2(s_ref[b, :, lanes])
            d_hi, d_lo = _blockdiag(s_hi, diag), _blockdiag(s_lo, diag)
            a_hi, a_lo = _split2(jnp.concatenate([rp_ref[b, rows, lanes], m_ref[b, rows, lanes]], axis=0))
            prods.append(jnp.dot(jnp.concatenate([a_hi, a_hi, a_lo], axis=1),
                                 jnp.concatenate([d_hi, d_lo, d_hi], axis=0), preferred_element_type=F32))
        ys = [pr[:C] + yl_ref[b, rows, lanes] for (b, lanes), pr in zip(pairs, prods)]
        for (b, lanes), pr in zip(pairs, prods):
            s_ref[b, :, lanes] = pr[C:] + n_ref[b, rows, lanes]
        for (b, lanes), y in zip(pairs, ys):
            yc = y - _pair_sum(y, first) * (1.0 / HEAD_DIM)
            var = _pair_sum(yc * yc, first) * (1.0 / HEAD_DIM)
            yn = yc * lax.rsqrt(var + GN_EPS) * lnw_ref[:, lanes] + lnb_ref[:, lanes]
            rk = r_ref[b, rows, lanes] * k_ref[b, rows, lanes] * rk_ref[:, lanes]
            y_out[b, rows, lanes] = yn + _pair_sum(rk, first) * v_ref[b, rows, lanes]


def _wkv_scan(m, n, rp, yl, r, k, v, rk, lnw, lnb, B, tb=128):
    N, D = m.shape
    T = N // B
    blk = pl.BlockSpec((B, tb, D), lambda t: (0, t, 0))
    vec = pl.BlockSpec((1, D), lambda t: (0, 0))
    args = [a.reshape(B, T, D) for a in (m, n, rp, yl, r, k, v)]
    return pl.pallas_call(
        _wkv_scan_kernel,
        out_shape=jax.ShapeDtypeStruct((B, T, D), F32),
        grid=(T // tb,),
        in_specs=[blk] * 7 + [vec] * 3,
        out_specs=blk,
        scratch_shapes=[pltpu.VMEM((B, HEAD_DIM, D), F32)],
        compiler_params=_params(("arbitrary",)),
        name="wkv_scan",
    )(*args, rk, lnw, lnb).reshape(N, D)


def _post_mlp_kernel(final, channel_major, *refs):
    if final:
        (x_ref, y_ref, gm_ref, mod_ref, wo_ref, g_ref, up_ref, down_ref, fg_ref, o_ref) = refs
    else:
        (x_ref, y_ref, gm_ref, mod_ref, wo_ref, g_ref, up_ref, down_ref, o_ref) = refs
    gt1, sh2, sc2, gt2 = mod_ref[0, 2], mod_ref[0, 3], mod_ref[0, 4], mod_ref[0, 5]
    gated = y_ref[...] * gm_ref[...]
    if channel_major:
        mix = _bdot_tn(gated, wo_ref[...])
    else:
        mix = _bdot(gated, wo_ref[...])
    x = x_ref[...] + (1.0 + gt1) * mix
    h = _rms_mod(x, g_ref[...], sh2, sc2)
    u = jnp.maximum(_bdot(h, up_ref[...]), 0.0)
    x = x + (1.0 + gt2) * _bdot(u * u, down_ref[...])
    if final:
        ms = jnp.mean(x * x, axis=-1, keepdims=True)
        x = x * lax.rsqrt(ms + NORM_EPS) * fg_ref[...]
    o_ref[...] = x


def _post_mlp(x, y, gmul, mod, wo, norm_g, w_up, w_down, T, channel_major, final_g=None, tm=TOKEN_TILE):
    N, D = x.shape
    tpb = T // tm
    final = final_g is not None
    tile = pl.BlockSpec((tm, D), lambda i: (i, 0))
    mixer = pl.BlockSpec((D, tm), lambda i: (0, i)) if channel_major else tile
    mods = pl.BlockSpec((1, N_MOD, 1, D), lambda i: (i // tpb, 0, 0, 0))
    args = [x, y, gmul, mod, wo, norm_g, w_up, w_down]
    specs = [tile, mixer, mixer, mods, _const_spec(wo.shape), _const_spec((1, D)),
             _const_spec(w_up.shape), _const_spec(w_down.shape)]
    if final:
        args.append(final_g)
        specs.append(_const_spec((1, D)))
    return pl.pallas_call(
        functools.partial(_post_mlp_kernel, final, channel_major),
        out_shape=jax.ShapeDtypeStruct((N, D), F32),
        grid=(N // tm,),
        in_specs=specs,
        out_specs=tile,
        compiler_params=_params(("parallel",)),
        name="post_mlp",
    )(*args)


def _shared_kv_kernel(tiles_per_batch, x_ref, mod_ref, g_ref, wk_ref, wvt_ref, wf_ref, fb_ref, kg_ref,
                      ind_ref, indt_ref, place_ref, k_out, vt_out, f_out, kf_out, carry_ref):
    i = pl.program_id(0)

    @pl.when(i % tiles_per_batch == 0)
    def _():
        carry_ref[...] = jnp.zeros_like(carry_ref)

    h = _rms_mod(x_ref[...], g_ref[...], mod_ref[0, 0], mod_ref[0, 1])
    tm = h.shape[0]
    k = _bdot(h, wk_ref[...])
    ss = _dot_sel_rhs(k * k, ind_ref[...])
    ms = _dot_sel_rhs(ss, indt_ref[...]) * (1.0 / HEAD_DIM)
    k_out[...] = (k * lax.rsqrt(ms + NORM_EPS) * kg_ref[...]).astype(BF16)
    vt_out[...] = _bdot_nt(wvt_ref[...], h).astype(BF16)
    log_f = jax.nn.log_sigmoid(_bdot(h, wf_ref[...]) + fb_ref[...])
    row = lax.broadcasted_iota(jnp.int32, (tm, tm), 0)
    col = lax.broadcasted_iota(jnp.int32, (tm, tm), 1)
    cum = _dot_sel_lhs((row >= col).astype(F32), log_f) + carry_ref[...]
    f_out[...] = cum
    carry_ref[...] = cum[tm - 1:tm, :]
    pieces = jnp.concatenate(_split3(cum * LOG2E), axis=1)
    kf_out[...] = jnp.dot(pieces, place_ref[...], preferred_element_type=F32).astype(BF16)


def _bias_placement(H, D):
    heads = LANE // HEAD_DIM
    r = jnp.arange(F_PIECES * LANE, dtype=jnp.int32)
    piece, head = r // LANE, r % LANE
    target = (head // heads) * LANE + (head % heads) * F_PIECES + piece
    hit = (target[:, None] == jnp.arange(D, dtype=jnp.int32)[None, :]) & (head < H)[:, None]
    return hit.astype(BF16)


def _shared_kv(x, mod, norm_g, wk, wvt, wf, fb, kg, ind, indt, T, tm=TOKEN_TILE):
    N, D = x.shape
    tpb = T // tm
    place = _bias_placement(D // HEAD_DIM, D)
    tile = pl.BlockSpec((tm, D), lambda i: (i, 0))
    mods = pl.BlockSpec((1, 2, 1, D), lambda i: (i // tpb, 0, 0, 0))
    return pl.pallas_call(
        functools.partial(_shared_kv_kernel, tpb),
        out_shape=(jax.ShapeDtypeStruct((N, D), BF16), jax.ShapeDtypeStruct((D, N), BF16),
                   jax.ShapeDtypeStruct((N, LANE), F32), jax.ShapeDtypeStruct((N, D), BF16)),
        grid=(N // tm,),
        in_specs=[tile, mods, _const_spec((1, D)), _const_spec(wk.shape), _const_spec(wvt.shape),
                  _const_spec(wf.shape), _const_spec((1, LANE)), _const_spec((1, D)),
                  _const_spec(ind.shape), _const_spec(indt.shape), _const_spec(place.shape)],
        out_specs=(tile, pl.BlockSpec((D, tm), lambda i: (0, i)), pl.BlockSpec((tm, LANE), lambda i: (i, 0)),
                   tile),
        scratch_shapes=[pltpu.VMEM((1, LANE), F32)],
        compiler_params=_params(("arbitrary",)),
        name="shared_kv",
    )(x, mod, norm_g, wk, wvt, wf, fb, kg, ind, indt, place)


def _fox_q_kernel(x_ref, mod_ref, g_ref, wqt_ref, wgt_ref, qg_ref, qt_out, sgt_out):
    h = _rms_mod(x_ref[...], g_ref[...], mod_ref[0, 0], mod_ref[0, 1])
    tm = h.shape[0]
    qt = _bdot_nt(wqt_ref[...], h)
    q3 = qt.reshape(qt.shape[0] // HEAD_DIM, HEAD_DIM, tm)
    q3 = q3 * lax.rsqrt(jnp.mean(q3 * q3, axis=1, keepdims=True) + NORM_EPS)
    qt_out[...] = (q3.reshape(qt.shape) * qg_ref[...] * (HEAD_DIM ** -0.5 * LOG2E)).astype(BF16)
    sgt_out[...] = jax.nn.sigmoid(_bdot_nt(wgt_ref[...], h))


def _fox_q(x, mod, norm_g, wqt, wgt, qg_rep, T):
    N, D = x.shape
    tm = qg_rep.shape[1]
    tpb = T // tm
    tile = pl.BlockSpec((tm, D), lambda i: (i, 0))
    ttile = pl.BlockSpec((D, tm), lambda i: (0, i))
    mods = pl.BlockSpec((1, N_MOD, 1, D), lambda i: (i // tpb, 0, 0, 0))
    return pl.pallas_call(
        _fox_q_kernel,
        out_shape=(jax.ShapeDtypeStruct((D, N), BF16), jax.ShapeDtypeStruct((D, N), F32)),
        grid=(N // tm,),
        in_specs=[tile, mods, _const_spec((1, D)), _const_spec(wqt.shape), _const_spec(wgt.shape),
                  _const_spec((D, tm))],
        out_specs=(ttile, ttile),
        compiler_params=_params(("parallel",)),
        name="fox_q",
    )(x, mod, norm_g, wqt, wgt, qg_rep)


def _fox_attn_kernel(tq, tk, sub, ahead, qt_ref, k_ref, kf_ref, vt_ref, ft_ref, ot_ref, z_ref):
    p = pl.program_id(1)
    qi = pl.program_id(2)
    heads = LANE // HEAD_DIM

    chan = lax.broadcasted_iota(jnp.int32, (2 * LANE, tq), 0)
    key_in_sub = lax.broadcasted_iota(jnp.int32, (sub, tq), 0)
    query_in_blk = lax.broadcasted_iota(jnp.int32, (sub, tq), 1)
    q = qt_ref[...]
    q2 = jnp.concatenate([q, jnp.zeros_like(q)], axis=0)
    qz = []
    for hh in range(heads):
        own = (chan >= hh * HEAD_DIM) & (chan < (hh + 1) * HEAD_DIM)
        bias = (chan >= LANE + hh * F_PIECES) & (chan < LANE + (hh + 1) * F_PIECES)
        qz.append(jnp.where(own, q2, jnp.where(bias, -1.0, 0.0).astype(q.dtype)))
    f_t = [ft_ref[0, pl.ds(p * heads + hh, 1), :] * LOG2E for hh in range(heads)]
    steps = [(u, hh) for u in range(tk // sub) for hh in range(heads)]

    n_steps, n_slots = len(steps), ahead + 1
    assert n_steps % n_slots == 0 and ahead <= n_steps

    def scores(j, idx, masked):
        u, hh = steps[idx]
        start = pl.multiple_of(j * tk + u * sub, sub)
        keys = jnp.concatenate([k_ref[pl.ds(start, sub), :], kf_ref[pl.ds(start, sub), :]], axis=1)
        z = jnp.dot(keys, qz[hh], preferred_element_type=F32)
        if masked:
            z = jnp.where(start + key_in_sub <= qi * tq + query_in_blk, z, -jnp.inf)
        z_ref[idx % n_slots] = z

    def tile(j, carry, masked, next_masked):
        state = list(carry)
        for idx, (u, hh) in enumerate(steps):
            if idx + ahead < n_steps:
                scores(j, idx + ahead, masked)
            elif next_masked is not None:
                scores(j + 1, idx + ahead - n_steps, next_masked)
            z = z_ref[idx % n_slots]
            m, acc = state[hh]
            m_new = jnp.maximum(m, jnp.max(z, axis=0, keepdims=True) + f_t[hh])
            alpha = jnp.exp2(m - m_new)
            e = jnp.exp2((z - (m_new - f_t[hh])).astype(BF16))
            start = pl.multiple_of(j * tk + u * sub, sub)
            v = vt_ref[hh * HEAD_DIM:(hh + 1) * HEAD_DIM, pl.ds(start, sub)]
            v1 = jnp.concatenate([v, ones_rows], axis=0)
            acc = alpha * acc + jnp.dot(v1, e, preferred_element_type=F32)
            state[hh] = (m_new, acc)
        return tuple(state)

    ones_rows = jnp.ones((DENOM_ROWS, sub), BF16)
    init = (jnp.full((1, tq), -jnp.inf, F32), jnp.zeros((HEAD_DIM + DENOM_ROWS, tq), F32))
    n_full = qi * (tq // tk)
    n_diag = tq // tk
    for i in range(ahead):
        scores(0, i, True)
    carry = lax.fori_loop(0, n_full - 1, lambda j, c: tile(j, c, False, False), (init,) * heads)
    carry = lax.fori_loop(jnp.maximum(n_full - 1, 0), n_full, lambda j, c: tile(j, c, False, True), carry)
    for d in range(n_diag):
        carry = tile(n_full + d, carry, True, True if d + 1 < n_diag else None)
    for hh in range(heads):
        m, acc = carry[hh]
        ot_ref[hh * HEAD_DIM:(hh + 1) * HEAD_DIM, :] = acc[:HEAD_DIM] / acc[HEAD_DIM:HEAD_DIM + 1]


def _fox_attn(qt, k, kf, vt, ft, B, tq=512, tk=512, sub=256, ahead=3):
    D, N = qt.shape
    T = N // B
    nq = T // tq
    qblk = pl.BlockSpec((LANE, tq), lambda b, p, i: (p, b * nq + i))
    keys = pl.BlockSpec((T, LANE), lambda b, p, i: (b, p))
    return pl.pallas_call(
        functools.partial(_fox_attn_kernel, tq, tk, sub, ahead),
        out_shape=jax.ShapeDtypeStruct((D, N), F32),
        grid=(B, D // LANE, nq),
        in_specs=[qblk, keys, keys,
                  pl.BlockSpec((LANE, T), lambda b, p, i: (p, b)),
                  pl.BlockSpec((1, ft.shape[1], tq), lambda b, p, i: (b, 0, i))],
        out_specs=qblk,
        scratch_shapes=[pltpu.VMEM((ahead + 1, sub, tq), F32)],
        compiler_params=_params(("parallel", "parallel", "arbitrary")),
        name="fox_attn",
    )(qt, k, kf, vt, ft)


def kernel(x, c, mod_w, mod_b, norm_mix_g, norm_mlp_g, mlp_up, mlp_down, rw_mu, rw_wr, rw_wk, rw_wv, rw_wo, rw_w0, rw_w1, rw_w2, rw_a0, rw_a1, rw_a2, rw_g1, rw_g2, rw_kk, rw_ka, rw_rk, rw_lnw, rw_lnb, rw_v0, rw_v1, rw_v2, kv_norm_g, kv_mod_w, kv_mod_b, kv_w, kv_fb, kv_kg, fx_wqg, fx_qg, fx_wo, final_g):
    B, T, D = x.shape
    depth = mod_w.shape[0]
    n_a = rw_wr.shape[0]
    H = D // HEAD_DIM
    N = B * T
    bf = lambda w: w.astype(BF16)
    row = lambda vct: vct.reshape(1, -1)

    c_pad = jnp.pad(c, ((0, SUBLANE - B), (0, 0)))
    mods = _modulation(c_pad, mod_w, mod_b)[:, :B].reshape(depth, B, N_MOD, 1, D)
    kv_mod = _modulation(c_pad, kv_mod_w[None], kv_mod_b[None])[0, :B].reshape(B, 2, 1, D)

    head_of = jnp.arange(D, dtype=jnp.int32) // HEAD_DIM
    ind = (head_of[:, None] == jnp.arange(LANE, dtype=jnp.int32)[None, :]).astype(F32)
    indt = ind.T

    xf = x.reshape(N, D)
    v_first = None
    k_sh = kf_sh = vt_sh = ft_sh = None
    for i in range(depth):
        if i < n_a:
            p = dict(mu=rw_mu[i], wr=bf(rw_wr[i]), wk=bf(rw_wk[i]), wv=bf(rw_wv[i]),
                     w0=row(rw_w0[i]), w1=bf(rw_w1[i]), w2=bf(rw_w2[i]),
                     a0=row(rw_a0[i]), a1=bf(rw_a1[i]), a2=bf(rw_a2[i]),
                     g1=bf(rw_g1[i]), g2=bf(rw_g2[i]), kk=row(rw_kk[i]), ka=row(rw_ka[i]))
            if i > 0:
                p.update(v0=row(rw_v0[i - 1]), v1=bf(rw_v1[i - 1]), v2=bf(rw_v2[i - 1]))
            r, k, v, lw, kk, a, gate = _rwkv_pre(xf, mods[i], row(norm_mix_g[i]), p,
                                                 v_first if i > 0 else None, T)
            if i == 0:
                v_first = v
            m, n, rp, yl = _wkv_prep(r, k, v, lw, kk, a)
            y = _wkv_scan(m, n, rp, yl, r, k, v, row(rw_rk[i]), row(rw_lnw[i]), row(rw_lnb[i]), B)
            gmul, wo = gate, bf(rw_wo[i])
        else:
            j = i - n_a
            if j == 0:
                wf = jnp.pad(kv_w[:, 2 * D:], ((0, 0), (0, LANE - H)))
                fb = jnp.pad(kv_fb, (0, LANE - H)).reshape(1, LANE)
                k_sh, vt_sh, f_sh, kf_sh = _shared_kv(xf, kv_mod, row(kv_norm_g), bf(kv_w[:, :D]),
                                                      bf(kv_w[:, D:2 * D].T), bf(wf), fb,
                                                      row(jnp.tile(kv_kg, H)), ind, indt, T)
                ft_sh = f_sh[:, :H].reshape(B, T, H).transpose(0, 2, 1)
            qg_rep = jnp.broadcast_to(jnp.tile(fx_qg[j], H)[:, None], (D, TOKEN_TILE))
            qt, gmul = _fox_q(xf, mods[i], row(norm_mix_g[i]), bf(fx_wqg[j][:, :D].T), bf(fx_wqg[j][:, D:].T),
                              qg_rep, T)
            y = _fox_attn(qt, k_sh, kf_sh, vt_sh, ft_sh, B)
            wo = bf(fx_wo[j])
        xf = _post_mlp(xf, y, gmul, mods[i], wo, row(norm_mlp_g[i]), bf(mlp_up[i]), bf(mlp_down[i]), T, i >= n_a,
                       final_g=row(final_g) if i == depth - 1 else None)
    return xf.reshape(B, T, D)
```

```python
import functools

import jax
import jax.numpy as jnp
from jax import lax
from jax.experimental import pallas as pl
from jax.experimental.pallas import tpu as pltpu

F32 = jnp.float32
BF16 = jnp.bfloat16

HEAD_DIM = 64
N_MOD = 6
NORM_EPS = 1e-6
GN_EPS = 64e-5
L2_EPS = 1e-12
WKV_CHUNK = 64
LANE = 128
SUBLANE = 8
VMEM_LIMIT = 56 * 1024 * 1024
TOKEN_TILE = 256
LOG2E = 1.4426950408889634
DENOM_ROWS = 16
F_PIECES = 3


def _bdot(a, b):
    return jnp.dot(a.astype(BF16), b.astype(BF16), preferred_element_type=F32)


def _bdot_nt(a, b):
    return lax.dot_general(a.astype(BF16), b.astype(BF16), (((1,), (1,)), ((), ())),
                           preferred_element_type=F32)


def _bdot_tn(a, b):
    return lax.dot_general(a.astype(BF16), b.astype(BF16), (((0,), (0,)), ((), ())),
                           preferred_element_type=F32)


def _fdot(a, b):
    return jnp.dot(a, b, preferred_element_type=F32, precision=lax.Precision.HIGHEST)


def _split2(x):
    hi = x.astype(BF16)
    return hi, (x - hi.astype(F32)).astype(BF16)


def _split3(x):
    hi = x.astype(BF16)
    rest = x - hi.astype(F32)
    mid = rest.astype(BF16)
    lo = (rest - mid.astype(F32)).astype(BF16)
    return hi, mid, lo


def _dot_sel_lhs(sel, x):
    n = x.shape[1]
    y = jnp.dot(sel.astype(BF16), jnp.concatenate(_split3(x), axis=1), preferred_element_type=F32)
    return y[:, :n] + y[:, n:2 * n] + y[:, 2 * n:]


def _dot_sel_rhs(x, sel):
    m = x.shape[0]
    y = jnp.dot(jnp.concatenate(_split3(x), axis=0), sel.astype(BF16), preferred_element_type=F32)
    return y[:m] + y[m:2 * m] + y[2 * m:]


def _rms_mod(x, g, shift, scale):
    ms = jnp.mean(x * x, axis=-1, keepdims=True)
    y = x * lax.rsqrt(ms + NORM_EPS) * g
    return y * (1.0 + scale) + shift


def _const_spec(shape):
    n = len(shape)
    return pl.BlockSpec(shape, lambda *_: (0,) * n, pipeline_mode=pl.Buffered(1))


def _params(sem):
    return pltpu.CompilerParams(dimension_semantics=sem, vmem_limit_bytes=VMEM_LIMIT)


def _mod_kernel(c_ref, w_ref, b_ref, o_ref):
    c = c_ref[...]
    ca = c * jax.nn.sigmoid(c)
    o_ref[0] = _fdot(ca, w_ref[0]) + b_ref[0]


def _modulation(c_pad, w, b, tn=1024):
    L, D, M = w.shape
    return pl.pallas_call(
        _mod_kernel,
        out_shape=jax.ShapeDtypeStruct((L, SUBLANE, M), F32),
        grid=(L, M // tn),
        in_specs=[pl.BlockSpec((SUBLANE, D), lambda l, j: (0, 0)),
                  pl.BlockSpec((1, D, tn), lambda l, j: (l, 0, j)),
                  pl.BlockSpec((1, 1, tn), lambda l, j: (l, 0, j))],
        out_specs=pl.BlockSpec((1, SUBLANE, tn), lambda l, j: (l, 0, j)),
        compiler_params=_params(("parallel", "parallel")),
        name="modulation",
    )(c_pad, w, b.reshape(L, 1, M))


def _rwkv_pre_kernel(has_vres, tiles_per_batch, *refs):
    if has_vres:
        (x_ref, xp_ref, mod_ref, g_ref, mu_ref, wr_ref, wk_ref, wv_ref, w0_ref, w1_ref, w2_ref,
         a0_ref, a1_ref, a2_ref, g1_ref, g2_ref, kk_ref, ka_ref,
         vf_ref, v0_ref, v1_ref, v2_ref,
         r_out, k_out, v_out, lw_out, kk_out, a_out, g_out) = refs
    else:
        (x_ref, xp_ref, mod_ref, g_ref, mu_ref, wr_ref, wk_ref, wv_ref, w0_ref, w1_ref, w2_ref,
         a0_ref, a1_ref, a2_ref, g1_ref, g2_ref, kk_ref, ka_ref,
         r_out, k_out, v_out, lw_out, kk_out, a_out, g_out) = refs
    i = pl.program_id(0)
    shift, scale = mod_ref[0, 0], mod_ref[0, 1]
    g = g_ref[...]
    h = _rms_mod(x_ref[...], g, shift, scale)
    tm = h.shape[0]
    h_last = _rms_mod(xp_ref[SUBLANE - 1:SUBLANE, :], g, shift, scale)
    h_last = jnp.where(i % tiles_per_batch == 0, 0.0, h_last)
    row = lax.broadcasted_iota(jnp.int32, (tm, 1), 0)
    h_prev = jnp.where(row == 0, h_last, pltpu.roll(h, 1, axis=0))
    xx = h_prev - h
    xr, xw, xk, xv, xa, xg = (h + xx * mu_ref[j:j + 1, :] for j in range(6))

    r = _bdot(xr, wr_ref[...])
    k = _bdot(xk, wk_ref[...])
    v = _bdot(xv, wv_ref[...])
    w_log = -jax.nn.softplus(-(w0_ref[...] + _bdot(jnp.tanh(_bdot(xw, w1_ref[...])), w2_ref[...]))) - 0.5
    if has_vres:
        mix = jax.nn.sigmoid(v0_ref[...] + _bdot(_bdot(xv, v1_ref[...]), v2_ref[...]))
        v = v + (vf_ref[...] - v) * mix
    a = jax.nn.sigmoid(a0_ref[...] + _bdot(_bdot(xa, a1_ref[...]), a2_ref[...]))
    gate = _bdot(jax.nn.sigmoid(_bdot(xg, g1_ref[...])), g2_ref[...])

    r_out[...] = r
    k_out[...] = k * (1.0 + (a - 1.0) * ka_ref[...])
    v_out[...] = v
    lw_out[...] = -jnp.exp(w_log)
    kk_out[...] = k * kk_ref[...]
    a_out[...] = a
    g_out[...] = gate


def _rwkv_pre(x, mod, norm_g, p, v_first, T, tm=TOKEN_TILE):
    N, D = x.shape
    tpb = T // tm
    has_vres = v_first is not None
    tile = pl.BlockSpec((tm, D), lambda i: (i, 0))
    prev = pl.BlockSpec((SUBLANE, D), lambda i: (jnp.maximum(i * (tm // SUBLANE) - 1, 0), 0))
    mods = pl.BlockSpec((1, N_MOD, 1, D), lambda i: (i // tpb, 0, 0, 0))
    vec = _const_spec((1, D))
    args = [x, x, mod, norm_g, p["mu"], p["wr"], p["wk"], p["wv"], p["w0"], p["w1"], p["w2"],
            p["a0"], p["a1"], p["a2"], p["g1"], p["g2"], p["kk"], p["ka"]]
    specs = [tile, prev, mods, vec, _const_spec(p["mu"].shape)]
    specs += [_const_spec(a.shape) for a in args[5:]]
    if has_vres:
        args += [v_first, p["v0"], p["v1"], p["v2"]]
        specs += [tile, vec, _const_spec(p["v1"].shape), _const_spec(p["v2"].shape)]
    out = jax.ShapeDtypeStruct((N, D), F32)
    return pl.pallas_call(
        functools.partial(_rwkv_pre_kernel, has_vres, tpb),
        out_shape=(out,) * 7,
        grid=(N // tm,),
        in_specs=specs,
        out_specs=(tile,) * 7,
        compiler_params=_params(("parallel",)),
        name="rwkv_pre",
    )(*args)


def _blockdiag(w, diag):
    return jnp.where(diag, jnp.concatenate([w, w], axis=0), jnp.zeros((), w.dtype))


def _pair_mm(l, rs, diag):
    rhs = jnp.concatenate([_blockdiag(r.astype(BF16), diag) for r in rs], axis=1)
    return jnp.dot(l.astype(BF16), rhs, preferred_element_type=F32)


def _pair_mm_nt(l, rs, diag):
    rhs = jnp.concatenate([_blockdiag(r.astype(BF16), diag) for r in rs], axis=0)
    return lax.dot_general(l.astype(BF16), rhs, (((1,), (1,)), ((), ())), preferred_element_type=F32)


def _pair_mm_tn(l, rs, first):
    rhs = jnp.concatenate([r.astype(BF16) for r in rs], axis=1)
    full = lax.dot_general(l.astype(BF16), rhs, (((0,), (0,)), ((), ())), preferred_element_type=F32)
    return [jnp.where(first, full[:HEAD_DIM, i * LANE:(i + 1) * LANE], full[HEAD_DIM:, i * LANE:(i + 1) * LANE])
            for i in range(len(rs))]


def _pair_masks(C):
    row = lax.broadcasted_iota(jnp.int32, (C, LANE), 0)
    lane = lax.broadcasted_iota(jnp.int32, (C, LANE), 1)
    col = lane & (HEAD_DIM - 1)
    first = lane < HEAD_DIM
    row2 = lax.broadcasted_iota(jnp.int32, (LANE, LANE), 0)
    lane2 = lax.broadcasted_iota(jnp.int32, (LANE, LANE), 1)
    diag = (row2 < HEAD_DIM) == (lane2 < HEAD_DIM)
    return row, col, first, diag


def _pair_sum(x, first):
    s0 = jnp.sum(jnp.where(first, x, 0.0), axis=-1, keepdims=True)
    s1 = jnp.sum(jnp.where(first, 0.0, x), axis=-1, keepdims=True)
    return jnp.where(first, s0, s1)


def _wkv_prep_kernel(r_ref, k_ref, v_ref, lw_ref, kk_ref, a_ref, rk_ref,
                     m_out, n_out, rp_out, yl_out, bonus_out):
    C = WKV_CHUNK
    chunks = [slice(ci * C, (ci + 1) * C) for ci in range(r_ref.shape[0] // C)]
    row, col, first, diag = _pair_masks(C)
    tri_incl, tri_strict, eye = row >= col, row > col, row == col
    cum_op = (lax.broadcasted_iota(jnp.int32, (C, C), 0) >= lax.broadcasted_iota(jnp.int32, (C, C), 1)).astype(F32)

    def same_block(s):
        return (row ^ col) < s
    levels, s = [], 8
    while s < C:
        levels.append((same_block(s), same_block(2 * s)))
        s *= 2

    at, rt, bt, kt, bh, kh, w_end, vs = [], [], [], [], [], [], [], []
    for rows in chunks:
        lw, kkr = lw_ref[rows, :], kk_ref[rows, :]
        kk = kkr / jnp.maximum(jnp.sqrt(_pair_sum(kkr * kkr, first)), L2_EPS)
        bvec = kk * a_ref[rows, :]
        cw = _dot_sel_lhs(cum_op, lw)
        cw_end = cw[C - 1:C, :]
        w_inv, w_rest = jnp.exp(-cw), jnp.exp(cw_end - cw)
        k = k_ref[rows, :]
        at.append(-kk * jnp.exp(cw - lw))
        rt.append(r_ref[rows, :] * jnp.exp(cw))
        bt.append(bvec * w_inv)
        kt.append(k * w_inv)
        bh.append(bvec * w_rest)
        kh.append(k * w_rest)
        w_end.append(jnp.exp(cw_end))
        vs.append(v_ref[rows, :])
    n = range(len(chunks))

    amat = [_pair_mm_nt(jnp.concatenate([at[i], rt[i]], axis=0), [bt[i], kt[i]], diag) for i in n]
    a_ab = [jnp.where(tri_strict, amat[i][:C, :LANE], 0.0) for i in n]
    a_ak = [jnp.where(tri_strict, amat[i][:C, LANE:], 0.0) for i in n]
    a_rb = [jnp.where(tri_incl, amat[i][C:, :LANE], 0.0) for i in n]
    a_rk = [jnp.where(tri_incl, amat[i][C:, LANE:], 0.0) for i in n]
    akv = [_pair_mm(a_ak[i], [vs[i]], diag) for i in n]

    blk8 = same_block(8)
    x = [jnp.where(blk8, a_ab[i], 0.0) for i in n]
    inv = [jnp.where(eye, 1.0, x[i]) for i in n]
    for _ in range(2):
        x = [_pair_mm(x[i], [x[i]], diag) for i in n]
        inv = [inv[i] + _pair_mm(x[i], [inv[i]], diag) for i in n]
    for inner, outer in levels:
        off = [jnp.where(outer, jnp.where(inner, 0.0, a_ab[i]), 0.0) for i in n]
        t = [_pair_mm(off[i], [inv[i]], diag) for i in n]
        inv = [inv[i] + _pair_mm(inv[i], [t[i]], diag) for i in n]

    sol = [_pair_mm(inv[i], [at[i], akv[i]], diag) for i in n]
    ap = [sol[i][:, :LANE] for i in n]
    ul = [sol[i][:, LANE:] for i in n]
    for i in n:
        rows = chunks[i]
        rb = _pair_mm(a_rb[i], [ap[i], ul[i]], diag)
        rp_out[rows, :] = rt[i] + rb[:, :LANE]
        yl_out[rows, :] = rb[:, LANE:] + _pair_mm(a_rk[i], [vs[i]], diag)
        bh_ap, bh_ul = _pair_mm_tn(bh[i], [ap[i], ul[i]], first)
        m_out[rows, :] = jnp.where(eye, w_end[i], 0.0) + bh_ap
        n_out[rows, :] = bh_ul + _pair_mm_tn(kh[i], [vs[i]], first)[0]
        bonus_out[rows, :] = _pair_sum(r_ref[rows, :] * k_ref[rows, :] * rk_ref[...], first) * vs[i]


def _wkv_prep(r, k, v, lw, kk, a, rk, tb=1024):
    N, D = r.shape
    assert WKV_CHUNK == HEAD_DIM and 2 * HEAD_DIM == LANE
    blk = pl.BlockSpec((tb, LANE), lambda i, p: (i, p))
    out = jax.ShapeDtypeStruct((N, D), F32)
    return pl.pallas_call(
        _wkv_prep_kernel,
        out_shape=(out,) * 5,
        grid=(N // tb, D // LANE),
        in_specs=[blk] * 6 + [pl.BlockSpec((1, LANE), lambda i, p: (0, p))],
        out_specs=(blk,) * 5,
        compiler_params=_params(("parallel", "parallel")),
        name="wkv_prep",
    )(r, k, v, lw, kk, a, rk)


def _wkv_scan_kernel(m_ref, n_ref, rp_ref, yl_ref, bonus_ref, lnw_ref, lnb_ref, y_out, s_ref):
    C = WKV_CHUNK
    B, tb, D = m_ref.shape
    _, _, _, diag = _pair_masks(C)
    head_ones = diag.astype(F32)

    @pl.when(pl.program_id(0) == 0)
    def _():
        s_ref[...] = jnp.zeros_like(s_ref)

    pairs = [(b, slice(p * LANE, (p + 1) * LANE)) for b in range(B) for p in range(D // LANE)]
    for ci in range(tb // C):
        rows = slice(ci * C, (ci + 1) * C)
        prods = []
        for b, lanes in pairs:
            s_hi, s_lo = _split2(s_ref[b, :, lanes])
            d_hi, d_lo = _blockdiag(s_hi, diag), _blockdiag(s_lo, diag)
            a_hi, a_lo = _split2(jnp.concatenate([rp_ref[b, rows, lanes], m_ref[b, rows, lanes]], axis=0))
            prods.append(jnp.dot(jnp.concatenate([a_hi, a_hi, a_lo], axis=1),
                                 jnp.concatenate([d_hi, d_lo, d_hi], axis=0), preferred_element_type=F32))
        ys = [pr[:C] + yl_ref[b, rows, lanes] for (b, lanes), pr in zip(pairs, prods)]
        for (b, lanes), pr in zip(pairs, prods):
            s_ref[b, :, lanes] = pr[C:] + n_ref[b, rows, lanes]
        ycs = [y - _bdot(y, head_ones) * (1.0 / HEAD_DIM) for y in ys]
        vrs = [_bdot(yc * yc, head_ones) * (1.0 / HEAD_DIM) for yc in ycs]
        for (b, lanes), yc, var in zip(pairs, ycs, vrs):
            yn = yc * lax.rsqrt(var + GN_EPS) * lnw_ref[:, lanes] + lnb_ref[:, lanes]
            y_out[b, rows, lanes] = yn + bonus_ref[b, rows, lanes]


def _wkv_scan(m, n, rp, yl, bonus, lnw, lnb, B, tb=128):
    N, D = m.shape
    T = N // B
    blk = pl.BlockSpec((B, tb, D), lambda t: (0, t, 0))
    vec = pl.BlockSpec((1, D), lambda t: (0, 0))
    args = [a.reshape(B, T, D) for a in (m, n, rp, yl, bonus)]
    return pl.pallas_call(
        _wkv_scan_kernel,
        out_shape=jax.ShapeDtypeStruct((B, T, D), F32),
        grid=(T // tb,),
        in_specs=[blk] * 5 + [vec] * 2,
        out_specs=blk,
        scratch_shapes=[pltpu.VMEM((B, HEAD_DIM, D), F32)],
        compiler_params=_params(("arbitrary",)),
        name="wkv_scan",
    )(*args, lnw, lnb).reshape(N, D)


def _post_mlp_kernel(final, channel_major, *refs):
    if final:
        (x_ref, y_ref, gm_ref, mod_ref, wo_ref, g_ref, up_ref, down_ref, fg_ref, o_ref) = refs
    else:
        (x_ref, y_ref, gm_ref, mod_ref, wo_ref, g_ref, up_ref, down_ref, o_ref) = refs
    gt1, sh2, sc2, gt2 = mod_ref[0, 2], mod_ref[0, 3], mod_ref[0, 4], mod_ref[0, 5]
    gated = y_ref[...] * gm_ref[...]
    if channel_major:
        mix = _bdot_tn(gated, wo_ref[...])
    else:
        mix = _bdot(gated, wo_ref[...])
    x = x_ref[...] + (1.0 + gt1) * mix
    h = _rms_mod(x, g_ref[...], sh2, sc2)
    u = jnp.maximum(_bdot(h, up_ref[...]), 0.0)
    x = x + (1.0 + gt2) * _bdot(u * u, down_ref[...])
    if final:
        ms = jnp.mean(x * x, axis=-1, keepdims=True)
        x = x * lax.rsqrt(ms + NORM_EPS) * fg_ref[...]
    o_ref[...] = x


def _post_mlp(x, y, gmul, mod, wo, norm_g, w_up, w_down, T, channel_major, final_g=None, tm=TOKEN_TILE):
    N, D = x.shape
    tpb = T // tm
    final = final_g is not None
    tile = pl.BlockSpec((tm, D), lambda i: (i, 0))
    mixer = pl.BlockSpec((D, tm), lambda i: (0, i)) if channel_major else tile
    mods = pl.BlockSpec((1, N_MOD, 1, D), lambda i: (i // tpb, 0, 0, 0))
    args = [x, y, gmul, mod, wo, norm_g, w_up, w_down]
    specs = [tile, mixer, mixer, mods, _const_spec(wo.shape), _const_spec((1, D)),
             _const_spec(w_up.shape), _const_spec(w_down.shape)]
    if final:
        args.append(final_g)
        specs.append(_const_spec((1, D)))
    return pl.pallas_call(
        functools.partial(_post_mlp_kernel, final, channel_major),
        out_shape=jax.ShapeDtypeStruct((N, D), F32),
        grid=(N // tm,),
        in_specs=specs,
        out_specs=tile,
        compiler_params=_params(("parallel",)),
        name="post_mlp",
    )(*args)


def _shared_kv_kernel(tiles_per_batch, x_ref, mod_ref, g_ref, wk_ref, wvt_ref, wf_ref, fb_ref, kg_ref,
                      ind_ref, indt_ref, place_ref, k_out, vt_out, f_out, kf_out, carry_ref):
    i = pl.program_id(0)

    @pl.when(i % tiles_per_batch == 0)
    def _():
        carry_ref[...] = jnp.zeros_like(carry_ref)

    h = _rms_mod(x_ref[...], g_ref[...], mod_ref[0, 0], mod_ref[0, 1])
    tm = h.shape[0]
    k = _bdot(h, wk_ref[...])
    ss = _dot_sel_rhs(k * k, ind_ref[...])
    ms = _dot_sel_rhs(ss, indt_ref[...]) * (1.0 / HEAD_DIM)
    k_out[...] = (k * lax.rsqrt(ms + NORM_EPS) * kg_ref[...]).astype(BF16)
    vt_out[...] = _bdot_nt(wvt_ref[...], h).astype(BF16)
    log_f = jax.nn.log_sigmoid(_bdot(h, wf_ref[...]) + fb_ref[...])
    row = lax.broadcasted_iota(jnp.int32, (tm, tm), 0)
    col = lax.broadcasted_iota(jnp.int32, (tm, tm), 1)
    cum = _dot_sel_lhs((row >= col).astype(F32), log_f) + carry_ref[...]
    f_out[...] = cum
    carry_ref[...] = cum[tm - 1:tm, :]
    pieces = jnp.concatenate(_split3(cum * LOG2E), axis=1)
    kf_out[...] = jnp.dot(pieces, place_ref[...], preferred_element_type=F32).astype(BF16)


def _bias_placement(H, D):
    heads = LANE // HEAD_DIM
    r = jnp.arange(F_PIECES * LANE, dtype=jnp.int32)
    piece, head = r // LANE, r % LANE
    target = (head // heads) * LANE + (head % heads) * F_PIECES + piece
    hit = (target[:, None] == jnp.arange(D, dtype=jnp.int32)[None, :]) & (head < H)[:, None]
    return hit.astype(BF16)


def _shared_kv(x, mod, norm_g, wk, wvt, wf, fb, kg, ind, indt, T, tm=TOKEN_TILE):
    N, D = x.shape
    tpb = T // tm
    place = _bias_placement(D // HEAD_DIM, D)
    tile = pl.BlockSpec((tm, D), lambda i: (i, 0))
    mods = pl.BlockSpec((1, 2, 1, D), lambda i: (i // tpb, 0, 0, 0))
    return pl.pallas_call(
        functools.partial(_shared_kv_kernel, tpb),
        out_shape=(jax.ShapeDtypeStruct((N, D), BF16), jax.ShapeDtypeStruct((D, N), BF16),
                   jax.ShapeDtypeStruct((N, LANE), F32), jax.ShapeDtypeStruct((N, D), BF16)),
        grid=(N // tm,),
        in_specs=[tile, mods, _const_spec((1, D)), _const_spec(wk.shape), _const_spec(wvt.shape),
                  _const_spec(wf.shape), _const_spec((1, LANE)), _const_spec((1, D)),
                  _const_spec(ind.shape), _const_spec(indt.shape), _const_spec(place.shape)],
        out_specs=(tile, pl.BlockSpec((D, tm), lambda i: (0, i)), pl.BlockSpec((tm, LANE), lambda i: (i, 0)),
                   tile),
        scratch_shapes=[pltpu.VMEM((1, LANE), F32)],
        compiler_params=_params(("arbitrary",)),
        name="shared_kv",
    )(x, mod, norm_g, wk, wvt, wf, fb, kg, ind, indt, place)


def _fox_q_kernel(x_ref, mod_ref, g_ref, wqt_ref, wgt_ref, qg_ref, qt_out, sgt_out):
    h = _rms_mod(x_ref[...], g_ref[...], mod_ref[0, 0], mod_ref[0, 1])
    tm = h.shape[0]
    qt = _bdot_nt(wqt_ref[...], h)
    q3 = qt.reshape(qt.shape[0] // HEAD_DIM, HEAD_DIM, tm)
    q3 = q3 * lax.rsqrt(jnp.mean(q3 * q3, axis=1, keepdims=True) + NORM_EPS)
    qt_out[...] = (q3.reshape(qt.shape) * qg_ref[...] * (HEAD_DIM ** -0.5 * LOG2E)).astype(BF16)
    sgt_out[...] = jax.nn.sigmoid(_bdot_nt(wgt_ref[...], h))


def _fox_q(x, mod, norm_g, wqt, wgt, qg_rep, T):
    N, D = x.shape
    tm = qg_rep.shape[1]
    tpb = T // tm
    tile = pl.BlockSpec((tm, D), lambda i: (i, 0))
    ttile = pl.BlockSpec((D, tm), lambda i: (0, i))
    mods = pl.BlockSpec((1, N_MOD, 1, D), lambda i: (i // tpb, 0, 0, 0))
    return pl.pallas_call(
        _fox_q_kernel,
        out_shape=(jax.ShapeDtypeStruct((D, N), BF16), jax.ShapeDtypeStruct((D, N), F32)),
        grid=(N // tm,),
        in_specs=[tile, mods, _const_spec((1, D)), _const_spec(wqt.shape), _const_spec(wgt.shape),
                  _const_spec((D, tm))],
        out_specs=(ttile, ttile),
        compiler_params=_params(("parallel",)),
        name="fox_q",
    )(x, mod, norm_g, wqt, wgt, qg_rep)


def _fox_attn_kernel(tq, tk, sub, ahead, qt_ref, k_ref, kf_ref, vt_ref, ft_ref, ot_ref, z_ref):
    p = pl.program_id(1)
    qi = pl.program_id(2)
    heads = LANE // HEAD_DIM

    chan = lax.broadcasted_iota(jnp.int32, (2 * LANE, tq), 0)
    key_in_sub = lax.broadcasted_iota(jnp.int32, (sub, tq), 0)
    query_in_blk = lax.broadcasted_iota(jnp.int32, (sub, tq), 1)
    q = qt_ref[...]
    q2 = jnp.concatenate([q, jnp.zeros_like(q)], axis=0)
    qz = []
    for hh in range(heads):
        own = (chan >= hh * HEAD_DIM) & (chan < (hh + 1) * HEAD_DIM)
        bias = (chan >= LANE + hh * F_PIECES) & (chan < LANE + (hh + 1) * F_PIECES)
        qz.append(jnp.where(own, q2, jnp.where(bias, -1.0, 0.0).astype(q.dtype)))
    f_t = [ft_ref[0, pl.ds(p * heads + hh, 1), :] * LOG2E for hh in range(heads)]
    steps = [(u, hh) for u in range(tk // sub) for hh in range(heads)]

    n_steps, n_slots = len(steps), ahead + 1
    assert n_steps % n_slots == 0 and ahead <= n_steps

    def scores(j, idx, masked):
        u, hh = steps[idx]
        start = pl.multiple_of(j * tk + u * sub, sub)
        keys = jnp.concatenate([k_ref[pl.ds(start, sub), :], kf_ref[pl.ds(start, sub), :]], axis=1)
        z = jnp.dot(keys, qz[hh], preferred_element_type=F32)
        if masked:
            z = jnp.where(start + key_in_sub <= qi * tq + query_in_blk, z, -jnp.inf)
        z_ref[idx % n_slots] = z

    def tile(j, carry, masked, next_masked):
        state = list(carry)
        for idx, (u, hh) in enumerate(steps):
            if idx + ahead < n_steps:
                scores(j, idx + ahead, masked)
            elif next_masked is not None:
                scores(j + 1, idx + ahead - n_steps, next_masked)
            z = z_ref[idx % n_slots]
            m, acc = state[hh]
            m_new = jnp.maximum(m, jnp.max(z, axis=0, keepdims=True) + f_t[hh])
            alpha = jnp.exp2(m - m_new)
            e = jnp.exp2((z - (m_new - f_t[hh])).astype(BF16))
            start = pl.multiple_of(j * tk + u * sub, sub)
            v = vt_ref[hh * HEAD_DIM:(hh + 1) * HEAD_DIM, pl.ds(start, sub)]
            v1 = jnp.concatenate([v, ones_rows], axis=0)
            acc = alpha * acc + jnp.dot(v1, e, preferred_element_type=F32)
            state[hh] = (m_new, acc)
        return tuple(state)

    ones_rows = jnp.ones((DENOM_ROWS, sub), BF16)
    init = (jnp.full((1, tq), -jnp.inf, F32), jnp.zeros((HEAD_DIM + DENOM_ROWS, tq), F32))
    n_full = qi * (tq // tk)
    n_diag = tq // tk
    for i in range(ahead):
        scores(0, i, True)
    n_pairs = jnp.maximum(lax.shift_right_arithmetic(n_full - 1, 1), 0)

    def pair(i, c):
        return tile(2 * i + 1, tile(2 * i, c, False, False), False, False)
    carry = lax.fori_loop(0, n_pairs, pair, (init,) * heads)
    carry = lax.fori_loop(2 * n_pairs, n_full - 1, lambda j, c: tile(j, c, False, False), carry)
    carry = lax.fori_loop(jnp.maximum(n_full - 1, 0), n_full, lambda j, c: tile(j, c, False, True), carry)
    for d in range(n_diag):
        carry = tile(n_full + d, carry, True, True if d + 1 < n_diag else None)
    for hh in range(heads):
        m, acc = carry[hh]
        ot_ref[hh * HEAD_DIM:(hh + 1) * HEAD_DIM, :] = acc[:HEAD_DIM] / acc[HEAD_DIM:HEAD_DIM + 1]


def _fox_attn(qt, k, kf, vt, ft, B, tq=512, tk=512, sub=256, ahead=3):
    D, N = qt.shape
    T = N // B
    nq = T // tq
    qblk = pl.BlockSpec((LANE, tq), lambda b, p, i: (p, b * nq + i))
    keys = pl.BlockSpec((T, LANE), lambda b, p, i: (b, p))
    return pl.pallas_call(
        functools.partial(_fox_attn_kernel, tq, tk, sub, ahead),
        out_shape=jax.ShapeDtypeStruct((D, N), F32),
        grid=(B, D // LANE, nq),
        in_specs=[qblk, keys, keys,
                  pl.BlockSpec((LANE, T), lambda b, p, i: (p, b)),
                  pl.BlockSpec((1, ft.shape[1], tq), lambda b, p, i: (b, 0, i))],
        out_specs=qblk,
        scratch_shapes=[pltpu.VMEM((ahead + 1, sub, tq), F32)],
        compiler_params=_params(("parallel", "parallel", "arbitrary")),
        name="fox_attn",
    )(qt, k, kf, vt, ft)


def kernel(x, c, mod_w, mod_b, norm_mix_g, norm_mlp_g, mlp_up, mlp_down, rw_mu, rw_wr, rw_wk, rw_wv, rw_wo, rw_w0, rw_w1, rw_w2, rw_a0, rw_a1, rw_a2, rw_g1, rw_g2, rw_kk, rw_ka, rw_rk, rw_lnw, rw_lnb, rw_v0, rw_v1, rw_v2, kv_norm_g, kv_mod_w, kv_mod_b, kv_w, kv_fb, kv_kg, fx_wqg, fx_qg, fx_wo, final_g):
    B, T, D = x.shape
    depth = mod_w.shape[0]
    n_a = rw_wr.shape[0]
    H = D // HEAD_DIM
    N = B * T
    bf = lambda w: w.astype(BF16)
    row = lambda vct: vct.reshape(1, -1)

    c_pad = jnp.pad(c, ((0, SUBLANE - B), (0, 0)))
    mods = _modulation(c_pad, mod_w, mod_b)[:, :B].reshape(depth, B, N_MOD, 1, D)
    kv_mod = _modulation(c_pad, kv_mod_w[None], kv_mod_b[None])[0, :B].reshape(B, 2, 1, D)

    head_of = jnp.arange(D, dtype=jnp.int32) // HEAD_DIM
    ind = (head_of[:, None] == jnp.arange(LANE, dtype=jnp.int32)[None, :]).astype(F32)
    indt = ind.T

    xf = x.reshape(N, D)
    v_first = None
    k_sh = kf_sh = vt_sh = ft_sh = None
    for i in range(depth):
        if i < n_a:
            p = dict(mu=rw_mu[i], wr=bf(rw_wr[i]), wk=bf(rw_wk[i]), wv=bf(rw_wv[i]),
                     w0=row(rw_w0[i]), w1=bf(rw_w1[i]), w2=bf(rw_w2[i]),
                     a0=row(rw_a0[i]), a1=bf(rw_a1[i]), a2=bf(rw_a2[i]),
                     g1=bf(rw_g1[i]), g2=bf(rw_g2[i]), kk=row(rw_kk[i]), ka=row(rw_ka[i]))
            if i > 0:
                p.update(v0=row(rw_v0[i - 1]), v1=bf(rw_v1[i - 1]), v2=bf(rw_v2[i - 1]))
            r, k, v, lw, kk, a, gate = _rwkv_pre(xf, mods[i], row(norm_mix_g[i]), p,
                                                 v_first if i > 0 else None, T)
            if i == 0:
                v_first = v
            m, n, rp, yl, bonus = _wkv_prep(r, k, v, lw, kk, a, row(rw_rk[i]))
            y = _wkv_scan(m, n, rp, yl, bonus, row(rw_lnw[i]), row(rw_lnb[i]), B)
            gmul, wo = gate, bf(rw_wo[i])
        else:
            j = i - n_a
            if j == 0:
                wf = jnp.pad(kv_w[:, 2 * D:], ((0, 0), (0, LANE - H)))
                fb = jnp.pad(kv_fb, (0, LANE - H)).reshape(1, LANE)
                k_sh, vt_sh, f_sh, kf_sh = _shared_kv(xf, kv_mod, row(kv_norm_g), bf(kv_w[:, :D]),
                                                      bf(kv_w[:, D:2 * D].T), bf(wf), fb,
                                                      row(jnp.tile(kv_kg, H)), ind, indt, T)
                ft_sh = f_sh[:, :H].reshape(B, T, H).transpose(0, 2, 1)
            qg_rep = jnp.broadcast_to(jnp.tile(fx_qg[j], H)[:, None], (D, TOKEN_TILE))
            qt, gmul = _fox_q(xf, mods[i], row(norm_mix_g[i]), bf(fx_wqg[j][:, :D].T), bf(fx_wqg[j][:, D:].T),
                              qg_rep, T)
            y = _fox_attn(qt, k_sh, kf_sh, vt_sh, ft_sh, B)
            wo = bf(fx_wo[j])
        xf = _post_mlp(xf, y, gmul, mods[i], wo, row(norm_mlp_g[i]), bf(mlp_up[i]), bf(mlp_down[i]), T, i >= n_a,
                       final_g=row(final_g) if i == depth - 1 else None)
    return xf.reshape(B, T, D)
```

```python
import functools

import jax
import jax.numpy as jnp
from jax import lax
from jax.experimental import pallas as pl
from jax.experimental.pallas import tpu as pltpu

F32 = jnp.float32
BF16 = jnp.bfloat16

HEAD_DIM = 64
N_MOD = 6
NORM_EPS = 1e-6
GN_EPS = 64e-5
L2_EPS = 1e-12
WKV_CHUNK = 64
LANE = 128
SUBLANE = 8
VMEM_LIMIT = 56 * 1024 * 1024
TOKEN_TILE = 256
LOG2E = 1.4426950408889634
DENOM_ROWS = 16
F_PIECES = 3


def _bdot(a, b):
    return jnp.dot(a.astype(BF16), b.astype(BF16), preferred_element_type=F32)


def _bdot_nt(a, b):
    return lax.dot_general(a.astype(BF16), b.astype(BF16), (((1,), (1,)), ((), ())),
                           preferred_element_type=F32)


def _bdot_tn(a, b):
    return lax.dot_general(a.astype(BF16), b.astype(BF16), (((0,), (0,)), ((), ())),
                           preferred_element_type=F32)


def _split2(x):
    hi = x.astype(BF16)
    return hi, (x - hi.astype(F32)).astype(BF16)


def _split3(x):
    hi = x.astype(BF16)
    rest = x - hi.astype(F32)
    mid = rest.astype(BF16)
    lo = (rest - mid.astype(F32)).astype(BF16)
    return hi, mid, lo


def _dot_sel_lhs(sel, x):
    n = x.shape[1]
    y = jnp.dot(sel.astype(BF16), jnp.concatenate(_split3(x), axis=1), preferred_element_type=F32)
    return y[:, :n] + y[:, n:2 * n] + y[:, 2 * n:]


def _dot_sel_rhs(x, sel):
    m = x.shape[0]
    y = jnp.dot(jnp.concatenate(_split3(x), axis=0), sel.astype(BF16), preferred_element_type=F32)
    return y[:m] + y[m:2 * m] + y[2 * m:]


def _rms_mod(x, g, shift, scale):
    ms = jnp.mean(x * x, axis=-1, keepdims=True)
    y = x * lax.rsqrt(ms + NORM_EPS) * g
    return y * (1.0 + scale) + shift


def _const_spec(shape):
    n = len(shape)
    return pl.BlockSpec(shape, lambda *_: (0,) * n, pipeline_mode=pl.Buffered(1))


def _params(sem):
    return pltpu.CompilerParams(dimension_semantics=sem, vmem_limit_bytes=VMEM_LIMIT)


def _mod_kernel(c_ref, w_ref, b_ref, o_ref):
    c = c_ref[...]
    ca = c * jax.nn.sigmoid(c)
    rows = jnp.concatenate(_split3(ca), axis=0)
    w_hi, w_lo = _split2(w_ref[0])
    y = jnp.dot(rows, w_hi, preferred_element_type=F32)
    y_lo = jnp.dot(rows[:2 * SUBLANE], w_lo, preferred_element_type=F32)
    o_ref[0] = (y[:SUBLANE] + y[SUBLANE:2 * SUBLANE] + y[2 * SUBLANE:]
                + y_lo[:SUBLANE] + y_lo[SUBLANE:] + b_ref[0])


def _modulation(c_pad, w, b, tn=1024):
    L, D, M = w.shape
    return pl.pallas_call(
        _mod_kernel,
        out_shape=jax.ShapeDtypeStruct((L, SUBLANE, M), F32),
        grid=(L, M // tn),
        in_specs=[pl.BlockSpec((SUBLANE, D), lambda l, j: (0, 0)),
                  pl.BlockSpec((1, D, tn), lambda l, j: (l, 0, j)),
                  pl.BlockSpec((1, 1, tn), lambda l, j: (l, 0, j))],
        out_specs=pl.BlockSpec((1, SUBLANE, tn), lambda l, j: (l, 0, j)),
        compiler_params=_params(("parallel", "parallel")),
        name="modulation",
    )(c_pad, w, b.reshape(L, 1, M))


def _rwkv_pre_kernel(has_vres, tiles_per_batch, *refs):
    if has_vres:
        (x_ref, xp_ref, mod_ref, g_ref, mu_ref, wr_ref, wk_ref, wv_ref, w0_ref, w1_ref, w2_ref,
         a0_ref, a1_ref, a2_ref, g1_ref, g2_ref, kk_ref, ka_ref,
         vf_ref, v0_ref, v1_ref, v2_ref,
         r_out, k_out, v_out, lw_out, kk_out, a_out, g_out) = refs
    else:
        (x_ref, xp_ref, mod_ref, g_ref, mu_ref, wr_ref, wk_ref, wv_ref, w0_ref, w1_ref, w2_ref,
         a0_ref, a1_ref, a2_ref, g1_ref, g2_ref, kk_ref, ka_ref,
         r_out, k_out, v_out, lw_out, kk_out, a_out, g_out) = refs
    i = pl.program_id(0)
    shift, scale = mod_ref[0, 0], mod_ref[0, 1]
    g = g_ref[...]
    h = _rms_mod(x_ref[...], g, shift, scale)
    tm = h.shape[0]
    h_last = _rms_mod(xp_ref[SUBLANE - 1:SUBLANE, :], g, shift, scale)
    h_last = jnp.where(i % tiles_per_batch == 0, 0.0, h_last)
    row = lax.broadcasted_iota(jnp.int32, (tm, 1), 0)
    h_prev = jnp.where(row == 0, h_last, pltpu.roll(h, 1, axis=0))
    xx = h_prev - h
    xr, xw, xk, xv, xa, xg = (h + xx * mu_ref[j:j + 1, :] for j in range(6))

    r = _bdot(xr, wr_ref[...])
    k = _bdot(xk, wk_ref[...])
    v = _bdot(xv, wv_ref[...])
    w_log = -jax.nn.softplus(-(w0_ref[...] + _bdot(jnp.tanh(_bdot(xw, w1_ref[...])), w2_ref[...]))) - 0.5
    if has_vres:
        mix = jax.nn.sigmoid(v0_ref[...] + _bdot(_bdot(xv, v1_ref[...]), v2_ref[...]))
        v = v + (vf_ref[...] - v) * mix
    a = jax.nn.sigmoid(a0_ref[...] + _bdot(_bdot(xa, a1_ref[...]), a2_ref[...]))
    gate = _bdot(jax.nn.sigmoid(_bdot(xg, g1_ref[...])), g2_ref[...])

    r_out[...] = r
    k_out[...] = k * (1.0 + (a - 1.0) * ka_ref[...])
    v_out[...] = v
    lw_out[...] = -jnp.exp(w_log)
    kk_out[...] = k * kk_ref[...]
    a_out[...] = a
    g_out[...] = gate


def _rwkv_pre(x, mod, norm_g, p, v_first, T, tm=TOKEN_TILE):
    N, D = x.shape
    tpb = T // tm
    has_vres = v_first is not None
    tile = pl.BlockSpec((tm, D), lambda i: (i, 0))
    prev = pl.BlockSpec((SUBLANE, D), lambda i: (jnp.maximum(i * (tm // SUBLANE) - 1, 0), 0))
    mods = pl.BlockSpec((1, N_MOD, 1, D), lambda i: (i // tpb, 0, 0, 0))
    vec = _const_spec((1, D))
    args = [x, x, mod, norm_g, p["mu"], p["wr"], p["wk"], p["wv"], p["w0"], p["w1"], p["w2"],
            p["a0"], p["a1"], p["a2"], p["g1"], p["g2"], p["kk"], p["ka"]]
    specs = [tile, prev, mods, vec, _const_spec(p["mu"].shape)]
    specs += [_const_spec(a.shape) for a in args[5:]]
    if has_vres:
        args += [v_first, p["v0"], p["v1"], p["v2"]]
        specs += [tile, vec, _const_spec(p["v1"].shape), _const_spec(p["v2"].shape)]
    out = jax.ShapeDtypeStruct((N, D), F32)
    return pl.pallas_call(
        functools.partial(_rwkv_pre_kernel, has_vres, tpb),
        out_shape=(out,) * 7,
        grid=(N // tm,),
        in_specs=specs,
        out_specs=(tile,) * 7,
        compiler_params=_params(("parallel",)),
        name="rwkv_pre",
    )(*args)


def _blockdiag(w, diag):
    return jnp.where(diag, jnp.concatenate([w, w], axis=0), jnp.zeros((), w.dtype))


def _pair_mm(l, rs, diag):
    rhs = jnp.concatenate([_blockdiag(r.astype(BF16), diag) for r in rs], axis=1)
    return jnp.dot(l.astype(BF16), rhs, preferred_element_type=F32)


def _pair_mm_nt(l, rs, diag):
    rhs = jnp.concatenate([_blockdiag(r.astype(BF16), diag) for r in rs], axis=0)
    return lax.dot_general(l.astype(BF16), rhs, (((1,), (1,)), ((), ())), preferred_element_type=F32)


def _pair_mm_tn(l, rs, first):
    rhs = jnp.concatenate([r.astype(BF16) for r in rs], axis=1)
    full = lax.dot_general(l.astype(BF16), rhs, (((0,), (0,)), ((), ())), preferred_element_type=F32)
    return [jnp.where(first, full[:HEAD_DIM, i * LANE:(i + 1) * LANE], full[HEAD_DIM:, i * LANE:(i + 1) * LANE])
            for i in range(len(rs))]


def _pair_masks(C):
    row = lax.broadcasted_iota(jnp.int32, (C, LANE), 0)
    lane = lax.broadcasted_iota(jnp.int32, (C, LANE), 1)
    col = lane & (HEAD_DIM - 1)
    first = lane < HEAD_DIM
    row2 = lax.broadcasted_iota(jnp.int32, (LANE, LANE), 0)
    lane2 = lax.broadcasted_iota(jnp.int32, (LANE, LANE), 1)
    diag = (row2 < HEAD_DIM) == (lane2 < HEAD_DIM)
    return row, col, first, diag


def _pair_sum(x, first):
    s0 = jnp.sum(jnp.where(first, x, 0.0), axis=-1, keepdims=True)
    s1 = jnp.sum(jnp.where(first, 0.0, x), axis=-1, keepdims=True)
    return jnp.where(first, s0, s1)


def _wkv_prep_kernel(r_ref, k_ref, v_ref, lw_ref, kk_ref, a_ref, rk_ref,
                     m_out, n_out, rp_out, yl_out, bonus_out):
    C = WKV_CHUNK
    chunks = [slice(ci * C, (ci + 1) * C) for ci in range(r_ref.shape[0] // C)]
    row, col, first, diag = _pair_masks(C)
    tri_incl, tri_strict, eye = row >= col, row > col, row == col
    cum_op = (lax.broadcasted_iota(jnp.int32, (C, C), 0) >= lax.broadcasted_iota(jnp.int32, (C, C), 1)).astype(F32)

    def same_block(s):
        return (row ^ col) < s
    levels, s = [], 8
    while s < C:
        levels.append((same_block(s), same_block(2 * s)))
        s *= 2

    at, rt, bt, kt, bh, kh, w_end, vs = [], [], [], [], [], [], [], []
    for rows in chunks:
        lw, kkr = lw_ref[rows, :], kk_ref[rows, :]
        kk = kkr / jnp.maximum(jnp.sqrt(_pair_sum(kkr * kkr, first)), L2_EPS)
        bvec = kk * a_ref[rows, :]
        cw = _dot_sel_lhs(cum_op, lw)
        cw_end = cw[C - 1:C, :]
        w_inv, w_rest = jnp.exp(-cw), jnp.exp(cw_end - cw)
        k = k_ref[rows, :]
        at.append(-kk * jnp.exp(cw - lw))
        rt.append(r_ref[rows, :] * jnp.exp(cw))
        bt.append(bvec * w_inv)
        kt.append(k * w_inv)
        bh.append(bvec * w_rest)
        kh.append(k * w_rest)
        w_end.append(jnp.exp(cw_end))
        vs.append(v_ref[rows, :])
    n = range(len(chunks))

    amat = [_pair_mm_nt(jnp.concatenate([at[i], rt[i]], axis=0), [bt[i], kt[i]], diag) for i in n]
    a_ab = [jnp.where(tri_strict, amat[i][:C, :LANE], 0.0) for i in n]
    a_ak = [jnp.where(tri_strict, amat[i][:C, LANE:], 0.0) for i in n]
    a_rb = [jnp.where(tri_incl, amat[i][C:, :LANE], 0.0) for i in n]
    a_rk = [jnp.where(tri_incl, amat[i][C:, LANE:], 0.0) for i in n]
    akv = [_pair_mm(a_ak[i], [vs[i]], diag) for i in n]

    blk8 = same_block(8)
    x = [jnp.where(blk8, a_ab[i], 0.0) for i in n]
    inv = [jnp.where(eye, 1.0, x[i]) for i in n]
    for _ in range(2):
        x = [_pair_mm(x[i], [x[i]], diag) for i in n]
        inv = [inv[i] + _pair_mm(x[i], [inv[i]], diag) for i in n]
    for inner, outer in levels:
        off = [jnp.where(outer, jnp.where(inner, 0.0, a_ab[i]), 0.0) for i in n]
        t = [_pair_mm(off[i], [inv[i]], diag) for i in n]
        inv = [inv[i] + _pair_mm(inv[i], [t[i]], diag) for i in n]

    sol = [_pair_mm(inv[i], [at[i], akv[i]], diag) for i in n]
    ap = [sol[i][:, :LANE] for i in n]
    ul = [sol[i][:, LANE:] for i in n]
    for i in n:
        rows = chunks[i]
        rb = _pair_mm(a_rb[i], [ap[i], ul[i]], diag)
        rp_out[rows, :] = rt[i] + rb[:, :LANE]
        yl_out[rows, :] = rb[:, LANE:] + _pair_mm(a_rk[i], [vs[i]], diag)
        bh_ap, bh_ul = _pair_mm_tn(bh[i], [ap[i], ul[i]], first)
        m_out[rows, :] = jnp.where(eye, w_end[i], 0.0) + bh_ap
        n_out[rows, :] = bh_ul + _pair_mm_tn(kh[i], [vs[i]], first)[0]
        bonus_out[rows, :] = _pair_sum(r_ref[rows, :] * k_ref[rows, :] * rk_ref[...], first) * vs[i]


def _wkv_prep(r, k, v, lw, kk, a, rk, tb=1024):
    N, D = r.shape
    assert WKV_CHUNK == HEAD_DIM and 2 * HEAD_DIM == LANE
    blk = pl.BlockSpec((tb, LANE), lambda i, p: (i, p))
    out = jax.ShapeDtypeStruct((N, D), F32)
    return pl.pallas_call(
        _wkv_prep_kernel,
        out_shape=(out,) * 5,
        grid=(N // tb, D // LANE),
        in_specs=[blk] * 6 + [pl.BlockSpec((1, LANE), lambda i, p: (0, p))],
        out_specs=(blk,) * 5,
        compiler_params=_params(("parallel", "parallel")),
        name="wkv_prep",
    )(r, k, v, lw, kk, a, rk)


def _wkv_scan_kernel(m_ref, n_ref, rp_ref, yl_ref, bonus_ref, lnw_ref, lnb_ref, y_out, s_ref):
    C = WKV_CHUNK
    B, tb, D = m_ref.shape
    _, _, _, diag = _pair_masks(C)
    head_ones = diag.astype(F32)

    @pl.when(pl.program_id(0) == 0)
    def _():
        s_ref[...] = jnp.zeros_like(s_ref)

    pairs = [(b, slice(p * LANE, (p + 1) * LANE)) for b in range(B) for p in range(D // LANE)]
    for ci in range(tb // C):
        rows = slice(ci * C, (ci + 1) * C)
        prods = []
        for b, lanes in pairs:
            s_hi, s_lo = _split2(s_ref[b, :, lanes])
            d_hi, d_lo = _blockdiag(s_hi, diag), _blockdiag(s_lo, diag)
            a_hi, a_lo = _split2(jnp.concatenate([rp_ref[b, rows, lanes], m_ref[b, rows, lanes]], axis=0))
            prods.append(jnp.dot(jnp.concatenate([a_hi, a_hi, a_lo], axis=1),
                                 jnp.concatenate([d_hi, d_lo, d_hi], axis=0), preferred_element_type=F32))
        ys = [pr[:C] + yl_ref[b, rows, lanes] for (b, lanes), pr in zip(pairs, prods)]
        for (b, lanes), pr in zip(pairs, prods):
            s_ref[b, :, lanes] = pr[C:] + n_ref[b, rows, lanes]
        ycs = [y - _bdot(y, head_ones) * (1.0 / HEAD_DIM) for y in ys]
        vrs = [_bdot(yc * yc, head_ones) * (1.0 / HEAD_DIM) for yc in ycs]
        for (b, lanes), yc, var in zip(pairs, ycs, vrs):
            yn = yc * lax.rsqrt(var + GN_EPS) * lnw_ref[:, lanes] + lnb_ref[:, lanes]
            y_out[b, rows, lanes] = yn + bonus_ref[b, rows, lanes]


def _wkv_scan(m, n, rp, yl, bonus, lnw, lnb, B, tb=128):
    N, D = m.shape
    T = N // B
    blk = pl.BlockSpec((B, tb, D), lambda t: (0, t, 0))
    vec = pl.BlockSpec((1, D), lambda t: (0, 0))
    args = [a.reshape(B, T, D) for a in (m, n, rp, yl, bonus)]
    return pl.pallas_call(
        _wkv_scan_kernel,
        out_shape=jax.ShapeDtypeStruct((B, T, D), F32),
        grid=(T // tb,),
        in_specs=[blk] * 5 + [vec] * 2,
        out_specs=blk,
        scratch_shapes=[pltpu.VMEM((B, HEAD_DIM, D), F32)],
        compiler_params=_params(("arbitrary",)),
        name="wkv_scan",
    )(*args, lnw, lnb).reshape(N, D)


def _post_mlp_kernel(final, channel_major, *refs):
    if final:
        (x_ref, y_ref, gm_ref, mod_ref, wo_ref, g_ref, up_ref, down_ref, fg_ref, o_ref) = refs
    else:
        (x_ref, y_ref, gm_ref, mod_ref, wo_ref, g_ref, up_ref, down_ref, o_ref) = refs
    gt1, sh2, sc2, gt2 = mod_ref[0, 2], mod_ref[0, 3], mod_ref[0, 4], mod_ref[0, 5]
    gated = y_ref[...] * gm_ref[...]
    if channel_major:
        mix = _bdot_tn(gated, wo_ref[...])
    else:
        mix = _bdot(gated, wo_ref[...])
    x = x_ref[...] + (1.0 + gt1) * mix
    h = _rms_mod(x, g_ref[...], sh2, sc2)
    u = jnp.maximum(_bdot(h, up_ref[...]), 0.0)
    x = x + (1.0 + gt2) * _bdot(u * u, down_ref[...])
    if final:
        ms = jnp.mean(x * x, axis=-1, keepdims=True)
        x = x * lax.rsqrt(ms + NORM_EPS) * fg_ref[...]
    o_ref[...] = x


def _post_mlp(x, y, gmul, mod, wo, norm_g, w_up, w_down, T, channel_major, final_g=None, tm=TOKEN_TILE):
    N, D = x.shape
    tpb = T // tm
    final = final_g is not None
    tile = pl.BlockSpec((tm, D), lambda i: (i, 0))
    mixer = pl.BlockSpec((D, tm), lambda i: (0, i)) if channel_major else tile
    mods = pl.BlockSpec((1, N_MOD, 1, D), lambda i: (i // tpb, 0, 0, 0))
    args = [x, y, gmul, mod, wo, norm_g, w_up, w_down]
    specs = [tile, mixer, mixer, mods, _const_spec(wo.shape), _const_spec((1, D)),
             _const_spec(w_up.shape), _const_spec(w_down.shape)]
    if final:
        args.append(final_g)
        specs.append(_const_spec((1, D)))
    return pl.pallas_call(
        functools.partial(_post_mlp_kernel, final, channel_major),
        out_shape=jax.ShapeDtypeStruct((N, D), F32),
        grid=(N // tm,),
        in_specs=specs,
        out_specs=tile,
        compiler_params=_params(("parallel",)),
        name="post_mlp",
    )(*args)


def _shared_kv_kernel(tiles_per_batch, x_ref, mod_ref, g_ref, wk_ref, wvt_ref, wf_ref, fb_ref, kg_ref,
                      ind_ref, indt_ref, place_ref, k_out, vt_out, f_out, kf_out, carry_ref):
    i = pl.program_id(0)

    @pl.when(i % tiles_per_batch == 0)
    def _():
        carry_ref[...] = jnp.zeros_like(carry_ref)

    h = _rms_mod(x_ref[...], g_ref[...], mod_ref[0, 0], mod_ref[0, 1])
    tm = h.shape[0]
    k = _bdot(h, wk_ref[...])
    ss = _dot_sel_rhs(k * k, ind_ref[...])
    ms = _dot_sel_rhs(ss, indt_ref[...]) * (1.0 / HEAD_DIM)
    k_out[...] = (k * lax.rsqrt(ms + NORM_EPS) * kg_ref[...]).astype(BF16)
    vt_out[...] = _bdot_nt(wvt_ref[...], h).astype(BF16)
    log_f = jax.nn.log_sigmoid(_bdot(h, wf_ref[...]) + fb_ref[...])
    row = lax.broadcasted_iota(jnp.int32, (tm, tm), 0)
    col = lax.broadcasted_iota(jnp.int32, (tm, tm), 1)
    cum = _dot_sel_lhs((row >= col).astype(F32), log_f) + carry_ref[...]
    f_out[...] = cum
    carry_ref[...] = cum[tm - 1:tm, :]
    pieces = jnp.concatenate(_split3(cum * LOG2E), axis=1)
    kf_out[...] = jnp.dot(pieces, place_ref[...], preferred_element_type=F32).astype(BF16)


def _bias_placement(H, D):
    heads = LANE // HEAD_DIM
    r = jnp.arange(F_PIECES * LANE, dtype=jnp.int32)
    piece, head = r // LANE, r % LANE
    target = (head // heads) * LANE + (head % heads) * F_PIECES + piece
    hit = (target[:, None] == jnp.arange(D, dtype=jnp.int32)[None, :]) & (head < H)[:, None]
    return hit.astype(BF16)


def _shared_kv(x, mod, norm_g, wk, wvt, wf, fb, kg, ind, indt, T, tm=TOKEN_TILE):
    N, D = x.shape
    tpb = T // tm
    place = _bias_placement(D // HEAD_DIM, D)
    tile = pl.BlockSpec((tm, D), lambda i: (i, 0))
    mods = pl.BlockSpec((1, 2, 1, D), lambda i: (i // tpb, 0, 0, 0))
    return pl.pallas_call(
        functools.partial(_shared_kv_kernel, tpb),
        out_shape=(jax.ShapeDtypeStruct((N, D), BF16), jax.ShapeDtypeStruct((D, N), BF16),
                   jax.ShapeDtypeStruct((N, LANE), F32), jax.ShapeDtypeStruct((N, D), BF16)),
        grid=(N // tm,),
        in_specs=[tile, mods, _const_spec((1, D)), _const_spec(wk.shape), _const_spec(wvt.shape),
                  _const_spec(wf.shape), _const_spec((1, LANE)), _const_spec((1, D)),
                  _const_spec(ind.shape), _const_spec(indt.shape), _const_spec(place.shape)],
        out_specs=(tile, pl.BlockSpec((D, tm), lambda i: (0, i)), pl.BlockSpec((tm, LANE), lambda i: (i, 0)),
                   tile),
        scratch_shapes=[pltpu.VMEM((1, LANE), F32)],
        compiler_params=_params(("arbitrary",)),
        name="shared_kv",
    )(x, mod, norm_g, wk, wvt, wf, fb, kg, ind, indt, place)


def _fox_q_kernel(x_ref, mod_ref, g_ref, wqt_ref, wgt_ref, qg_ref, qt_out, sgt_out):
    h = _rms_mod(x_ref[...], g_ref[...], mod_ref[0, 0], mod_ref[0, 1])
    tm = h.shape[0]
    qt = _bdot_nt(wqt_ref[...], h)
    q3 = qt.reshape(qt.shape[0] // HEAD_DIM, HEAD_DIM, tm)
    q3 = q3 * lax.rsqrt(jnp.mean(q3 * q3, axis=1, keepdims=True) + NORM_EPS)
    qt_out[...] = (q3.reshape(qt.shape) * qg_ref[...] * (HEAD_DIM ** -0.5 * LOG2E)).astype(BF16)
    sgt_out[...] = jax.nn.sigmoid(_bdot_nt(wgt_ref[...], h))


def _fox_q(x, mod, norm_g, wqt, wgt, qg_rep, T):
    N, D = x.shape
    tm = qg_rep.shape[1]
    tpb = T // tm
    tile = pl.BlockSpec((tm, D), lambda i: (i, 0))
    ttile = pl.BlockSpec((D, tm), lambda i: (0, i))
    mods = pl.BlockSpec((1, N_MOD, 1, D), lambda i: (i // tpb, 0, 0, 0))
    return pl.pallas_call(
        _fox_q_kernel,
        out_shape=(jax.ShapeDtypeStruct((D, N), BF16), jax.ShapeDtypeStruct((D, N), F32)),
        grid=(N // tm,),
        in_specs=[tile, mods, _const_spec((1, D)), _const_spec(wqt.shape), _const_spec(wgt.shape),
                  _const_spec((D, tm))],
        out_specs=(ttile, ttile),
        compiler_params=_params(("parallel",)),
        name="fox_q",
    )(x, mod, norm_g, wqt, wgt, qg_rep)


def _fox_attn_kernel(tq, tk, sub, ahead, cross, qt_ref, k_ref, kf_ref, vt_ref, ft_ref, ot_ref, z_ref, acc_ref):
    p = pl.program_id(1)
    qi = pl.program_id(2)
    heads = LANE // HEAD_DIM

    chan = lax.broadcasted_iota(jnp.int32, (2 * LANE, tq), 0)
    key_in_sub = lax.broadcasted_iota(jnp.int32, (sub, tq), 0)
    query_in_blk = lax.broadcasted_iota(jnp.int32, (sub, tq), 1)
    q = qt_ref[...]
    q2 = jnp.concatenate([q, jnp.zeros_like(q)], axis=0)
    qz = []
    for hh in range(heads):
        own = (chan >= hh * HEAD_DIM) & (chan < (hh + 1) * HEAD_DIM)
        bias = (chan >= LANE + hh * F_PIECES) & (chan < LANE + (hh + 1) * F_PIECES)
        qz.append(jnp.where(own, q2, jnp.where(bias, -1.0, 0.0).astype(q.dtype)))
    f_t = [ft_ref[0, pl.ds(p * heads + hh, 1), :] * LOG2E for hh in range(heads)]
    steps = [(u, hh) for u in range(tk // sub) for hh in range(heads)]

    n_steps, n_slots = len(steps), ahead + 1
    assert n_steps % n_slots == 0 and cross <= ahead <= n_steps

    def scores(j, idx, masked):
        u, hh = steps[idx]
        start = pl.multiple_of(j * tk + u * sub, sub)
        keys = jnp.concatenate([k_ref[pl.ds(start, sub), :], kf_ref[pl.ds(start, sub), :]], axis=1)
        z = jnp.dot(keys, qz[hh], preferred_element_type=F32)
        if masked:
            z = jnp.where(start + key_in_sub <= qi * tq + query_in_blk, z, -jnp.inf)
        z_ref[idx % n_slots] = z

    def trip(tiles, carry, next_tile):
        seq = [(j, idx, masked) for j, masked in tiles for idx in range(n_steps)]
        n_own = len(seq)
        if next_tile is not None:
            seq += [(next_tile[0], idx, next_tile[1]) for idx in range(cross)]
        for g in range(cross, min(ahead, len(seq))):
            scores(*seq[g])
        state = list(carry)
        for g in range(n_own):
            if g + ahead < len(seq):
                scores(*seq[g + ahead])
            j, idx, _ = seq[g]
            u, hh = steps[idx]
            z = z_ref[idx % n_slots]
            m = state[hh]
            m_new = jnp.maximum(m, jnp.max(z, axis=0, keepdims=True) + f_t[hh])
            alpha = jnp.exp2(m - m_new)
            e = jnp.exp2((z - (m_new - f_t[hh])).astype(BF16))
            start = pl.multiple_of(j * tk + u * sub, sub)
            v = vt_ref[hh * HEAD_DIM:(hh + 1) * HEAD_DIM, pl.ds(start, sub)]
            v1 = jnp.concatenate([v, ones_rows], axis=0)
            acc_ref[hh] = alpha * acc_ref[hh] + jnp.dot(v1, e, preferred_element_type=F32)
            state[hh] = m_new
        return tuple(state)

    ones_rows = jnp.ones((DENOM_ROWS, sub), BF16)
    acc_ref[...] = jnp.zeros_like(acc_ref)
    init = jnp.full((1, tq), -jnp.inf, F32)
    n_full = qi * (tq // tk)
    n_diag = tq // tk
    for i in range(cross):
        scores(0, i, True)
    n_pairs = lax.shift_right_logical(n_full, 1)
    carry = lax.fori_loop(
        0, n_pairs,
        lambda i, c: trip([(2 * i, False), (2 * i + 1, False)], c, (2 * i + 2, True)), (init,) * heads)
    carry = lax.fori_loop(2 * n_pairs, n_full, lambda j, c: trip([(j, False)], c, (j + 1, True)), carry)
    for d in range(n_diag):
        carry = trip([(n_full + d, True)], carry, (n_full + d + 1, True) if d + 1 < n_diag else None)
    for hh in range(heads):
        acc = acc_ref[hh]
        ot_ref[hh * HEAD_DIM:(hh + 1) * HEAD_DIM, :] = acc[:HEAD_DIM] / acc[HEAD_DIM:HEAD_DIM + 1]


def _fox_attn(qt, k, kf, vt, ft, B, tq=512, tk=512, sub=256, ahead=3, cross=2):
    D, N = qt.shape
    T = N // B
    nq = T // tq
    qblk = pl.BlockSpec((LANE, tq), lambda b, p, i: (p, b * nq + i))
    keys = pl.BlockSpec((T, LANE), lambda b, p, i: (b, p))
    return pl.pallas_call(
        functools.partial(_fox_attn_kernel, tq, tk, sub, ahead, cross),
        out_shape=jax.ShapeDtypeStruct((D, N), F32),
        grid=(B, D // LANE, nq),
        in_specs=[qblk, keys, keys,
                  pl.BlockSpec((LANE, T), lambda b, p, i: (p, b)),
                  pl.BlockSpec((1, ft.shape[1], tq), lambda b, p, i: (b, 0, i))],
        out_specs=qblk,
        scratch_shapes=[pltpu.VMEM((ahead + 1, sub, tq), F32),
                        pltpu.VMEM((LANE // HEAD_DIM, HEAD_DIM + DENOM_ROWS, tq), F32)],
        compiler_params=_params(("parallel", "parallel", "arbitrary")),
        name="fox_attn",
    )(qt, k, kf, vt, ft)


def kernel(x, c, mod_w, mod_b, norm_mix_g, norm_mlp_g, mlp_up, mlp_down, rw_mu, rw_wr, rw_wk, rw_wv, rw_wo, rw_w0, rw_w1, rw_w2, rw_a0, rw_a1, rw_a2, rw_g1, rw_g2, rw_kk, rw_ka, rw_rk, rw_lnw, rw_lnb, rw_v0, rw_v1, rw_v2, kv_norm_g, kv_mod_w, kv_mod_b, kv_w, kv_fb, kv_kg, fx_wqg, fx_qg, fx_wo, final_g):
    B, T, D = x.shape
    depth = mod_w.shape[0]
    n_a = rw_wr.shape[0]
    H = D // HEAD_DIM
    N = B * T
    bf = lambda w: w.astype(BF16)
    row = lambda vct: vct.reshape(1, -1)

    c_pad = jnp.pad(c, ((0, SUBLANE - B), (0, 0)))
    mods = _modulation(c_pad, mod_w, mod_b)[:, :B].reshape(depth, B, N_MOD, 1, D)
    kv_mod = _modulation(c_pad, kv_mod_w[None], kv_mod_b[None])[0, :B].reshape(B, 2, 1, D)

    head_of = jnp.arange(D, dtype=jnp.int32) // HEAD_DIM
    ind = (head_of[:, None] == jnp.arange(LANE, dtype=jnp.int32)[None, :]).astype(F32)
    indt = ind.T

    xf = x.reshape(N, D)
    v_first = None
    k_sh = kf_sh = vt_sh = ft_sh = None
    for i in range(depth):
        if i < n_a:
            p = dict(mu=rw_mu[i], wr=bf(rw_wr[i]), wk=bf(rw_wk[i]), wv=bf(rw_wv[i]),
                     w0=row(rw_w0[i]), w1=bf(rw_w1[i]), w2=bf(rw_w2[i]),
                     a0=row(rw_a0[i]), a1=bf(rw_a1[i]), a2=bf(rw_a2[i]),
                     g1=bf(rw_g1[i]), g2=bf(rw_g2[i]), kk=row(rw_kk[i]), ka=row(rw_ka[i]))
            if i > 0:
                p.update(v0=row(rw_v0[i - 1]), v1=bf(rw_v1[i - 1]), v2=bf(rw_v2[i - 1]))
            r, k, v, lw, kk, a, gate = _rwkv_pre(xf, mods[i], row(norm_mix_g[i]), p,
                                                 v_first if i > 0 else None, T)
            if i == 0:
                v_first = v
            m, n, rp, yl, bonus = _wkv_prep(r, k, v, lw, kk, a, row(rw_rk[i]))
            y = _wkv_scan(m, n, rp, yl, bonus, row(rw_lnw[i]), row(rw_lnb[i]), B)
            gmul, wo = gate, bf(rw_wo[i])
        else:
            j = i - n_a
            if j == 0:
                wf = jnp.pad(kv_w[:, 2 * D:], ((0, 0), (0, LANE - H)))
                fb = jnp.pad(kv_fb, (0, LANE - H)).reshape(1, LANE)
                k_sh, vt_sh, f_sh, kf_sh = _shared_kv(xf, kv_mod, row(kv_norm_g), bf(kv_w[:, :D]),
                                                      bf(kv_w[:, D:2 * D].T), bf(wf), fb,
                                                      row(jnp.tile(kv_kg, H)), ind, indt, T)
                ft_sh = f_sh[:, :H].reshape(B, T, H).transpose(0, 2, 1)
            qg_rep = jnp.broadcast_to(jnp.tile(fx_qg[j], H)[:, None], (D, TOKEN_TILE))
            qt, gmul = _fox_q(xf, mods[i], row(norm_mix_g[i]), bf(fx_wqg[j][:, :D].T), bf(fx_wqg[j][:, D:].T),
                              qg_rep, T)
            y = _fox_attn(qt, k_sh, kf_sh, vt_sh, ft_sh, B)
            wo = bf(fx_wo[j])
        xf = _post_mlp(xf, y, gmul, mods[i], wo, row(norm_mlp_g[i]), bf(mlp_up[i]), bf(mlp_down[i]), T, i >= n_a,
                       final_g=row(final_g) if i == depth - 1 else None)
    return xf.reshape(B, T, D)
```

```python
import functools

import jax
import jax.numpy as jnp
from jax import lax
from jax.experimental import pallas as pl
from jax.experimental.pallas import tpu as pltpu

F32 = jnp.float32
BF16 = jnp.bfloat16

HEAD_DIM = 64
N_MOD = 6
NORM_EPS = 1e-6
GN_EPS = 64e-5
L2_EPS = 1e-12
WKV_CHUNK = 64
LANE = 128
SUBLANE = 8
VMEM_LIMIT = 56 * 1024 * 1024

TOKEN_TILE = 256
MOD_COLS = 1024
WKV_PREP_TOKENS = 1024
WKV_SCAN_TOKENS = 128
ATTN_Q = 512
ATTN_K = 512
ATTN_SUB = 256
ATTN_AHEAD = 3
ATTN_CROSS = 2
ATTN_Q_BLOCKS = 2
LOG2E = 1.4426950408889634
DENOM_ROWS = 16
F_PIECES = 3


def _bdot(a, b):
    return jnp.dot(a.astype(BF16), b.astype(BF16), preferred_element_type=F32)


def _bdot_nt(a, b):
    return lax.dot_general(a.astype(BF16), b.astype(BF16), (((1,), (1,)), ((), ())),
                           preferred_element_type=F32)


def _bdot_tn(a, b):
    return lax.dot_general(a.astype(BF16), b.astype(BF16), (((0,), (0,)), ((), ())),
                           preferred_element_type=F32)


def _split2(x):
    hi = x.astype(BF16)
    return hi, (x - hi.astype(F32)).astype(BF16)


def _split3(x):
    hi = x.astype(BF16)
    rest = x - hi.astype(F32)
    mid = rest.astype(BF16)
    lo = (rest - mid.astype(F32)).astype(BF16)
    return hi, mid, lo


def _dot_sel_lhs(sel, x):
    n = x.shape[1]
    y = jnp.dot(sel.astype(BF16), jnp.concatenate(_split3(x), axis=1), preferred_element_type=F32)
    return y[:, :n] + y[:, n:2 * n] + y[:, 2 * n:]


def _dot_sel_rhs(x, sel):
    m = x.shape[0]
    y = jnp.dot(jnp.concatenate(_split3(x), axis=0), sel.astype(BF16), preferred_element_type=F32)
    return y[:m] + y[m:2 * m] + y[2 * m:]


def _rms_mod(x, g, shift, scale):
    ms = jnp.mean(x * x, axis=-1, keepdims=True)
    y = x * lax.rsqrt(ms + NORM_EPS) * g
    return y * (1.0 + scale) + shift


def _const_spec(shape):
    n = len(shape)
    return pl.BlockSpec(shape, lambda *_: (0,) * n, pipeline_mode=pl.Buffered(1))


def _params(sem):
    return pltpu.CompilerParams(dimension_semantics=sem, vmem_limit_bytes=VMEM_LIMIT)


def _mod_kernel(c_ref, w_ref, b_ref, o_ref):
    c = c_ref[...]
    ca = c * jax.nn.sigmoid(c)
    rows = jnp.concatenate(_split3(ca), axis=0)
    w_hi, w_lo = _split2(w_ref[0])
    y = jnp.dot(rows, w_hi, preferred_element_type=F32)
    y_lo = jnp.dot(rows[:2 * SUBLANE], w_lo, preferred_element_type=F32)
    o_ref[0] = (y[:SUBLANE] + y[SUBLANE:2 * SUBLANE] + y[2 * SUBLANE:]
                + y_lo[:SUBLANE] + y_lo[SUBLANE:] + b_ref[0])


def _modulation(c_pad, w, b, tn=MOD_COLS):
    L, D, M = w.shape
    return pl.pallas_call(
        _mod_kernel,
        out_shape=jax.ShapeDtypeStruct((L, SUBLANE, M), F32),
        grid=(L, M // tn),
        in_specs=[pl.BlockSpec((SUBLANE, D), lambda l, j: (0, 0)),
                  pl.BlockSpec((1, D, tn), lambda l, j: (l, 0, j)),
                  pl.BlockSpec((1, 1, tn), lambda l, j: (l, 0, j))],
        out_specs=pl.BlockSpec((1, SUBLANE, tn), lambda l, j: (l, 0, j)),
        compiler_params=_params(("parallel", "parallel")),
        name="modulation",
    )(c_pad, w, b.reshape(L, 1, M))


def _rwkv_pre_kernel(has_vres, tiles_per_batch, *refs):
    if has_vres:
        (x_ref, xp_ref, mod_ref, g_ref, mu_ref, wr_ref, wk_ref, wv_ref, w0_ref, w1_ref, w2_ref,
         a0_ref, a1_ref, a2_ref, g1_ref, g2_ref, kk_ref, ka_ref,
         vf_ref, v0_ref, v1_ref, v2_ref,
         r_out, k_out, v_out, lw_out, kk_out, a_out, g_out) = refs
    else:
        (x_ref, xp_ref, mod_ref, g_ref, mu_ref, wr_ref, wk_ref, wv_ref, w0_ref, w1_ref, w2_ref,
         a0_ref, a1_ref, a2_ref, g1_ref, g2_ref, kk_ref, ka_ref,
         r_out, k_out, v_out, lw_out, kk_out, a_out, g_out) = refs
    i = pl.program_id(0)
    shift, scale = mod_ref[0, 0], mod_ref[0, 1]
    g = g_ref[...]
    h = _rms_mod(x_ref[...], g, shift, scale)
    tm = h.shape[0]
    h_last = _rms_mod(xp_ref[SUBLANE - 1:SUBLANE, :], g, shift, scale)
    h_last = jnp.where(i % tiles_per_batch == 0, 0.0, h_last)
    row = lax.broadcasted_iota(jnp.int32, (tm, 1), 0)
    h_prev = jnp.where(row == 0, h_last, pltpu.roll(h, 1, axis=0))
    xx = h_prev - h
    xr, xw, xk, xv, xa, xg = (h + xx * mu_ref[j:j + 1, :] for j in range(6))

    r = _bdot(xr, wr_ref[...])
    k = _bdot(xk, wk_ref[...])
    v = _bdot(xv, wv_ref[...])
    w_log = -jax.nn.softplus(-(w0_ref[...] + _bdot(jnp.tanh(_bdot(xw, w1_ref[...])), w2_ref[...]))) - 0.5
    if has_vres:
        mix = jax.nn.sigmoid(v0_ref[...] + _bdot(_bdot(xv, v1_ref[...]), v2_ref[...]))
        v = v + (vf_ref[...] - v) * mix
    a = jax.nn.sigmoid(a0_ref[...] + _bdot(_bdot(xa, a1_ref[...]), a2_ref[...]))
    gate = _bdot(jax.nn.sigmoid(_bdot(xg, g1_ref[...])), g2_ref[...])

    r_out[...] = r
    k_out[...] = k * (1.0 + (a - 1.0) * ka_ref[...])
    v_out[...] = v
    lw_out[...] = -jnp.exp(w_log)
    kk_out[...] = k * kk_ref[...]
    a_out[...] = a
    g_out[...] = gate


def _rwkv_pre(x, mod, norm_g, p, v_first, T, tm=TOKEN_TILE):
    N, D = x.shape
    tpb = T // tm
    has_vres = v_first is not None
    tile = pl.BlockSpec((tm, D), lambda i: (i, 0))
    prev = pl.BlockSpec((SUBLANE, D), lambda i: (jnp.maximum(i * (tm // SUBLANE) - 1, 0), 0))
    mods = pl.BlockSpec((1, N_MOD, 1, D), lambda i: (i // tpb, 0, 0, 0))
    vec = _const_spec((1, D))
    args = [x, x, mod, norm_g, p["mu"], p["wr"], p["wk"], p["wv"], p["w0"], p["w1"], p["w2"],
            p["a0"], p["a1"], p["a2"], p["g1"], p["g2"], p["kk"], p["ka"]]
    specs = [tile, prev, mods, vec, _const_spec(p["mu"].shape)]
    specs += [_const_spec(a.shape) for a in args[5:]]
    if has_vres:
        args += [v_first, p["v0"], p["v1"], p["v2"]]
        specs += [tile, vec, _const_spec(p["v1"].shape), _const_spec(p["v2"].shape)]
    out = jax.ShapeDtypeStruct((N, D), F32)
    return pl.pallas_call(
        functools.partial(_rwkv_pre_kernel, has_vres, tpb),
        out_shape=(out,) * 7,
        grid=(N // tm,),
        in_specs=specs,
        out_specs=(tile,) * 7,
        compiler_params=_params(("parallel",)),
        name="rwkv_pre",
    )(*args)


def _blockdiag(w, diag):
    return jnp.where(diag, jnp.concatenate([w, w], axis=0), jnp.zeros((), w.dtype))


def _pair_mm(l, rs, diag):
    rhs = jnp.concatenate([_blockdiag(r.astype(BF16), diag) for r in rs], axis=1)
    return jnp.dot(l.astype(BF16), rhs, preferred_element_type=F32)


def _pair_mm_nt(l, rs, diag):
    rhs = jnp.concatenate([_blockdiag(r.astype(BF16), diag) for r in rs], axis=0)
    return lax.dot_general(l.astype(BF16), rhs, (((1,), (1,)), ((), ())), preferred_element_type=F32)


def _pair_mm_tn(l, rs, first):
    rhs = jnp.concatenate([r.astype(BF16) for r in rs], axis=1)
    full = lax.dot_general(l.astype(BF16), rhs, (((0,), (0,)), ((), ())), preferred_element_type=F32)
    return [jnp.where(first, full[:HEAD_DIM, i * LANE:(i + 1) * LANE], full[HEAD_DIM:, i * LANE:(i + 1) * LANE])
            for i in range(len(rs))]


def _pair_masks(C):
    row = lax.broadcasted_iota(jnp.int32, (C, LANE), 0)
    lane = lax.broadcasted_iota(jnp.int32, (C, LANE), 1)
    col = lane & (HEAD_DIM - 1)
    first = lane < HEAD_DIM
    row2 = lax.broadcasted_iota(jnp.int32, (LANE, LANE), 0)
    lane2 = lax.broadcasted_iota(jnp.int32, (LANE, LANE), 1)
    diag = (row2 < HEAD_DIM) == (lane2 < HEAD_DIM)
    return row, col, first, diag


def _pair_sum(x, first):
    s0 = jnp.sum(jnp.where(first, x, 0.0), axis=-1, keepdims=True)
    s1 = jnp.sum(jnp.where(first, 0.0, x), axis=-1, keepdims=True)
    return jnp.where(first, s0, s1)


def _wkv_prep_kernel(r_ref, k_ref, v_ref, lw_ref, kk_ref, a_ref, rk_ref,
                     m_out, n_out, rp_out, yl_out, bonus_out):
    C = WKV_CHUNK
    chunks = [slice(ci * C, (ci + 1) * C) for ci in range(r_ref.shape[0] // C)]
    row, col, first, diag = _pair_masks(C)
    tri_incl, tri_strict, eye = row >= col, row > col, row == col
    cum_op = (lax.broadcasted_iota(jnp.int32, (C, C), 0) >= lax.broadcasted_iota(jnp.int32, (C, C), 1)).astype(F32)

    def same_block(s):
        return (row ^ col) < s
    levels, s = [], 8
    while s < C:
        levels.append((same_block(s), same_block(2 * s)))
        s *= 2

    at, rt, bt, kt, bh, kh, w_end, vs = [], [], [], [], [], [], [], []
    for rows in chunks:
        lw, kkr = lw_ref[rows, :], kk_ref[rows, :]
        kk = kkr / jnp.maximum(jnp.sqrt(_pair_sum(kkr * kkr, first)), L2_EPS)
        bvec = kk * a_ref[rows, :]
        cw = _dot_sel_lhs(cum_op, lw)
        cw_end = cw[C - 1:C, :]
        w_inv, w_rest = jnp.exp(-cw), jnp.exp(cw_end - cw)
        k = k_ref[rows, :]
        at.append(-kk * jnp.exp(cw - lw))
        rt.append(r_ref[rows, :] * jnp.exp(cw))
        bt.append(bvec * w_inv)
        kt.append(k * w_inv)
        bh.append(bvec * w_rest)
        kh.append(k * w_rest)
        w_end.append(jnp.exp(cw_end))
        vs.append(v_ref[rows, :])
    n = range(len(chunks))

    amat = [_pair_mm_nt(jnp.concatenate([at[i], rt[i]], axis=0), [bt[i], kt[i]], diag) for i in n]
    a_ab = [jnp.where(tri_strict, amat[i][:C, :LANE], 0.0) for i in n]
    a_ak = [jnp.where(tri_strict, amat[i][:C, LANE:], 0.0) for i in n]
    a_rb = [jnp.where(tri_incl, amat[i][C:, :LANE], 0.0) for i in n]
    a_rk = [jnp.where(tri_incl, amat[i][C:, LANE:], 0.0) for i in n]
    akv = [_pair_mm(a_ak[i], [vs[i]], diag) for i in n]

    blk8 = same_block(8)
    x = [jnp.where(blk8, a_ab[i], 0.0) for i in n]
    inv = [jnp.where(eye, 1.0, x[i]) for i in n]
    for _ in range(2):
        x = [_pair_mm(x[i], [x[i]], diag) for i in n]
        inv = [inv[i] + _pair_mm(x[i], [inv[i]], diag) for i in n]
    for inner, outer in levels:
        off = [jnp.where(outer, jnp.where(inner, 0.0, a_ab[i]), 0.0) for i in n]
        t = [_pair_mm(off[i], [inv[i]], diag) for i in n]
        inv = [inv[i] + _pair_mm(inv[i], [t[i]], diag) for i in n]

    sol = [_pair_mm(inv[i], [at[i], akv[i]], diag) for i in n]
    ap = [sol[i][:, :LANE] for i in n]
    ul = [sol[i][:, LANE:] for i in n]
    for i in n:
        rows = chunks[i]
        rb = _pair_mm(a_rb[i], [ap[i], ul[i]], diag)
        rp_out[rows, :] = rt[i] + rb[:, :LANE]
        yl_out[rows, :] = rb[:, LANE:] + _pair_mm(a_rk[i], [vs[i]], diag)
        bh_ap, bh_ul = _pair_mm_tn(bh[i], [ap[i], ul[i]], first)
        m_out[rows, :] = jnp.where(eye, w_end[i], 0.0) + bh_ap
        n_out[rows, :] = bh_ul + _pair_mm_tn(kh[i], [vs[i]], first)[0]
        bonus_out[rows, :] = _pair_sum(r_ref[rows, :] * k_ref[rows, :] * rk_ref[...], first) * vs[i]


def _wkv_prep(r, k, v, lw, kk, a, rk, tb=WKV_PREP_TOKENS):
    N, D = r.shape
    assert WKV_CHUNK == HEAD_DIM and 2 * HEAD_DIM == LANE
    blk = pl.BlockSpec((tb, LANE), lambda i, p: (i, p))
    out = jax.ShapeDtypeStruct((N, D), F32)
    return pl.pallas_call(
        _wkv_prep_kernel,
        out_shape=(out,) * 5,
        grid=(N // tb, D // LANE),
        in_specs=[blk] * 6 + [pl.BlockSpec((1, LANE), lambda i, p: (0, p))],
        out_specs=(blk,) * 5,
        compiler_params=_params(("parallel", "parallel")),
        name="wkv_prep",
    )(r, k, v, lw, kk, a, rk)


def _wkv_scan_kernel(m_ref, n_ref, rp_ref, yl_ref, bonus_ref, lnw_ref, lnb_ref, y_out, s_ref):
    C = WKV_CHUNK
    B, tb, D = m_ref.shape
    _, _, _, diag = _pair_masks(C)
    head_ones = diag.astype(F32)

    @pl.when(pl.program_id(0) == 0)
    def _():
        s_ref[...] = jnp.zeros_like(s_ref)

    pairs = [(b, slice(p * LANE, (p + 1) * LANE)) for b in range(B) for p in range(D // LANE)]
    for ci in range(tb // C):
        rows = slice(ci * C, (ci + 1) * C)
        prods = []
        for b, lanes in pairs:
            s_hi, s_lo = _split2(s_ref[b, :, lanes])
            d_hi, d_lo = _blockdiag(s_hi, diag), _blockdiag(s_lo, diag)
            a_hi, a_lo = _split2(jnp.concatenate([rp_ref[b, rows, lanes], m_ref[b, rows, lanes]], axis=0))
            prods.append(jnp.dot(jnp.concatenate([a_hi, a_hi, a_lo], axis=1),
                                 jnp.concatenate([d_hi, d_lo, d_hi], axis=0), preferred_element_type=F32))
        ys = [pr[:C] + yl_ref[b, rows, lanes] for (b, lanes), pr in zip(pairs, prods)]
        for (b, lanes), pr in zip(pairs, prods):
            s_ref[b, :, lanes] = pr[C:] + n_ref[b, rows, lanes]
        ycs = [y - _bdot(y, head_ones) * (1.0 / HEAD_DIM) for y in ys]
        vrs = [_bdot(yc * yc, head_ones) * (1.0 / HEAD_DIM) for yc in ycs]
        for (b, lanes), yc, var in zip(pairs, ycs, vrs):
            yn = yc * lax.rsqrt(var + GN_EPS) * lnw_ref[:, lanes] + lnb_ref[:, lanes]
            y_out[b, rows, lanes] = yn + bonus_ref[b, rows, lanes]


def _wkv_scan(m, n, rp, yl, bonus, lnw, lnb, B, tb=WKV_SCAN_TOKENS):
    N, D = m.shape
    T = N // B
    blk = pl.BlockSpec((B, tb, D), lambda t: (0, t, 0))
    vec = pl.BlockSpec((1, D), lambda t: (0, 0))
    args = [a.reshape(B, T, D) for a in (m, n, rp, yl, bonus)]
    return pl.pallas_call(
        _wkv_scan_kernel,
        out_shape=jax.ShapeDtypeStruct((B, T, D), F32),
        grid=(T // tb,),
        in_specs=[blk] * 5 + [vec] * 2,
        out_specs=blk,
        scratch_shapes=[pltpu.VMEM((B, HEAD_DIM, D), F32)],
        compiler_params=_params(("arbitrary",)),
        name="wkv_scan",
    )(*args, lnw, lnb).reshape(N, D)


def _post_mlp_kernel(final, channel_major, *refs):
    if final:
        (x_ref, y_ref, gm_ref, mod_ref, wo_ref, g_ref, up_ref, down_ref, fg_ref, o_ref) = refs
    else:
        (x_ref, y_ref, gm_ref, mod_ref, wo_ref, g_ref, up_ref, down_ref, o_ref) = refs
    gt1, sh2, sc2, gt2 = mod_ref[0, 2], mod_ref[0, 3], mod_ref[0, 4], mod_ref[0, 5]
    gated = y_ref[...] * gm_ref[...]
    if channel_major:
        mix = _bdot_tn(gated, wo_ref[...])
    else:
        mix = _bdot(gated, wo_ref[...])
    x = x_ref[...] + (1.0 + gt1) * mix
    h = _rms_mod(x, g_ref[...], sh2, sc2)
    u = jnp.maximum(_bdot(h, up_ref[...]), 0.0)
    x = x + (1.0 + gt2) * _bdot(u * u, down_ref[...])
    if final:
        ms = jnp.mean(x * x, axis=-1, keepdims=True)
        x = x * lax.rsqrt(ms + NORM_EPS) * fg_ref[...]
    o_ref[...] = x


def _post_mlp(x, y, gmul, mod, wo, norm_g, w_up, w_down, T, channel_major, final_g=None, tm=TOKEN_TILE):
    N, D = x.shape
    tpb = T // tm
    final = final_g is not None
    tile = pl.BlockSpec((tm, D), lambda i: (i, 0))
    mixer = pl.BlockSpec((D, tm), lambda i: (0, i)) if channel_major else tile
    mods = pl.BlockSpec((1, N_MOD, 1, D), lambda i: (i // tpb, 0, 0, 0))
    args = [x, y, gmul, mod, wo, norm_g, w_up, w_down]
    specs = [tile, mixer, mixer, mods, _const_spec(wo.shape), _const_spec((1, D)),
             _const_spec(w_up.shape), _const_spec(w_down.shape)]
    if final:
        args.append(final_g)
        specs.append(_const_spec((1, D)))
    return pl.pallas_call(
        functools.partial(_post_mlp_kernel, final, channel_major),
        out_shape=jax.ShapeDtypeStruct((N, D), F32),
        grid=(N // tm,),
        in_specs=specs,
        out_specs=tile,
        compiler_params=_params(("parallel",)),
        name="post_mlp",
    )(*args)


def _shared_kv_kernel(tiles_per_batch, x_ref, mod_ref, g_ref, wk_ref, wvt_ref, wf_ref, fb_ref, kg_ref,
                      ind_ref, indt_ref, place_ref, k_out, vt_out, f_out, kf_out, carry_ref):
    i = pl.program_id(0)

    @pl.when(i % tiles_per_batch == 0)
    def _():
        carry_ref[...] = jnp.zeros_like(carry_ref)

    h = _rms_mod(x_ref[...], g_ref[...], mod_ref[0, 0], mod_ref[0, 1])
    tm = h.shape[0]
    k = _bdot(h, wk_ref[...])
    ss = _dot_sel_rhs(k * k, ind_ref[...])
    ms = _dot_sel_rhs(ss, indt_ref[...]) * (1.0 / HEAD_DIM)
    k_out[...] = (k * lax.rsqrt(ms + NORM_EPS) * kg_ref[...]).astype(BF16)
    vt_out[...] = _bdot_nt(wvt_ref[...], h).astype(BF16)
    log_f = jax.nn.log_sigmoid(_bdot(h, wf_ref[...]) + fb_ref[...])
    row = lax.broadcasted_iota(jnp.int32, (tm, tm), 0)
    col = lax.broadcasted_iota(jnp.int32, (tm, tm), 1)
    cum = _dot_sel_lhs((row >= col).astype(F32), log_f) + carry_ref[...]
    f_out[...] = cum
    carry_ref[...] = cum[tm - 1:tm, :]
    pieces = jnp.concatenate(_split3(cum * LOG2E), axis=1)
    kf_out[...] = jnp.dot(pieces, place_ref[...], preferred_element_type=F32).astype(BF16)


def _bias_placement(H, D):
    heads = LANE // HEAD_DIM
    r = jnp.arange(F_PIECES * LANE, dtype=jnp.int32)
    piece, head = r // LANE, r % LANE
    target = (head // heads) * LANE + (head % heads) * F_PIECES + piece
    hit = (target[:, None] == jnp.arange(D, dtype=jnp.int32)[None, :]) & (head < H)[:, None]
    return hit.astype(BF16)


def _shared_kv(x, mod, norm_g, wk, wvt, wf, fb, kg, ind, indt, T, tm=TOKEN_TILE):
    N, D = x.shape
    tpb = T // tm
    place = _bias_placement(D // HEAD_DIM, D)
    tile = pl.BlockSpec((tm, D), lambda i: (i, 0))
    mods = pl.BlockSpec((1, 2, 1, D), lambda i: (i // tpb, 0, 0, 0))
    return pl.pallas_call(
        functools.partial(_shared_kv_kernel, tpb),
        out_shape=(jax.ShapeDtypeStruct((N, D), BF16), jax.ShapeDtypeStruct((D, N), BF16),
                   jax.ShapeDtypeStruct((N, LANE), F32), jax.ShapeDtypeStruct((N, D), BF16)),
        grid=(N // tm,),
        in_specs=[tile, mods, _const_spec((1, D)), _const_spec(wk.shape), _const_spec(wvt.shape),
                  _const_spec(wf.shape), _const_spec((1, LANE)), _const_spec((1, D)),
                  _const_spec(ind.shape), _const_spec(indt.shape), _const_spec(place.shape)],
        out_specs=(tile, pl.BlockSpec((D, tm), lambda i: (0, i)), pl.BlockSpec((tm, LANE), lambda i: (i, 0)),
                   tile),
        scratch_shapes=[pltpu.VMEM((1, LANE), F32)],
        compiler_params=_params(("arbitrary",)),
        name="shared_kv",
    )(x, mod, norm_g, wk, wvt, wf, fb, kg, ind, indt, place)


def _fox_q_kernel(x_ref, mod_ref, g_ref, wqt_ref, wgt_ref, qg_ref, qt_out, sgt_out):
    h = _rms_mod(x_ref[...], g_ref[...], mod_ref[0, 0], mod_ref[0, 1])
    tm = h.shape[0]
    qt = _bdot_nt(wqt_ref[...], h)
    q3 = qt.reshape(qt.shape[0] // HEAD_DIM, HEAD_DIM, tm)
    q3 = q3 * lax.rsqrt(jnp.mean(q3 * q3, axis=1, keepdims=True) + NORM_EPS)
    qt_out[...] = (q3.reshape(qt.shape) * qg_ref[...] * (HEAD_DIM ** -0.5 * LOG2E)).astype(BF16)
    sgt_out[...] = jax.nn.sigmoid(_bdot_nt(wgt_ref[...], h))


def _fox_q(x, mod, norm_g, wqt, wgt, qg_rep, T):
    N, D = x.shape
    tm = qg_rep.shape[1]
    tpb = T // tm
    tile = pl.BlockSpec((tm, D), lambda i: (i, 0))
    ttile = pl.BlockSpec((D, tm), lambda i: (0, i))
    mods = pl.BlockSpec((1, N_MOD, 1, D), lambda i: (i // tpb, 0, 0, 0))
    return pl.pallas_call(
        _fox_q_kernel,
        out_shape=(jax.ShapeDtypeStruct((D, N), BF16), jax.ShapeDtypeStruct((D, N), F32)),
        grid=(N // tm,),
        in_specs=[tile, mods, _const_spec((1, D)), _const_spec(wqt.shape), _const_spec(wgt.shape),
                  _const_spec((D, tm))],
        out_specs=(ttile, ttile),
        compiler_params=_params(("parallel",)),
        name="fox_q",
    )(x, mod, norm_g, wqt, wgt, qg_rep)


def _fox_attn_kernel(tq, tk, sub, ahead, cross, q_blocks, qt_ref, k_ref, kf_ref, vt_ref, ft_ref, ot_ref,
                     z_ref, acc_ref):
    for blk in range(q_blocks):
        lanes = slice(blk * tq, (blk + 1) * tq)
        _fox_attn_block(tq, tk, sub, ahead, cross, pl.program_id(2) * q_blocks + blk,
                        qt_ref.at[:, lanes], k_ref, kf_ref, vt_ref, ft_ref.at[:, :, lanes],
                        ot_ref.at[:, lanes], z_ref, acc_ref)


def _fox_attn_block(tq, tk, sub, ahead, cross, qi, qt_ref, k_ref, kf_ref, vt_ref, ft_ref, ot_ref, z_ref, acc_ref):
    p = pl.program_id(1)
    heads = LANE // HEAD_DIM

    chan = lax.broadcasted_iota(jnp.int32, (2 * LANE, tq), 0)
    key_in_sub = lax.broadcasted_iota(jnp.int32, (sub, tq), 0)
    query_in_blk = lax.broadcasted_iota(jnp.int32, (sub, tq), 1)
    q = qt_ref[...]
    q2 = jnp.concatenate([q, jnp.zeros_like(q)], axis=0)
    qz = []
    for hh in range(heads):
        own = (chan >= hh * HEAD_DIM) & (chan < (hh + 1) * HEAD_DIM)
        bias = (chan >= LANE + hh * F_PIECES) & (chan < LANE + (hh + 1) * F_PIECES)
        qz.append(jnp.where(own, q2, jnp.where(bias, -1.0, 0.0).astype(q.dtype)))
    f_t = [ft_ref[0, pl.ds(p * heads + hh, 1), :] * LOG2E for hh in range(heads)]
    steps = [(u, hh) for u in range(tk // sub) for hh in range(heads)]

    n_steps, n_slots = len(steps), ahead + 1
    assert n_steps % n_slots == 0 and cross <= ahead <= n_steps

    def scores(j, idx, masked):
        u, hh = steps[idx]
        start = pl.multiple_of(j * tk + u * sub, sub)
        keys = jnp.concatenate([k_ref[pl.ds(start, sub), :], kf_ref[pl.ds(start, sub), :]], axis=1)
        z = jnp.dot(keys, qz[hh], preferred_element_type=F32)
        if masked:
            z = jnp.where(start + key_in_sub <= qi * tq + query_in_blk, z, -jnp.inf)
        z_ref[idx % n_slots] = z

    def trip(tiles, carry, next_tile):
        seq = [(j, idx, masked) for j, masked in tiles for idx in range(n_steps)]
        n_own = len(seq)
        if next_tile is not None:
            seq += [(next_tile[0], idx, next_tile[1]) for idx in range(cross)]
        for g in range(cross, min(ahead, len(seq))):
            scores(*seq[g])
        state = list(carry)
        for g in range(n_own):
            if g + ahead < len(seq):
                scores(*seq[g + ahead])
            j, idx, _ = seq[g]
            u, hh = steps[idx]
            z = z_ref[idx % n_slots]
            m = state[hh]
            m_new = jnp.maximum(m, jnp.max(z, axis=0, keepdims=True) + f_t[hh])
            alpha = jnp.exp2(m - m_new)
            e = jnp.exp2((z - (m_new - f_t[hh])).astype(BF16))
            start = pl.multiple_of(j * tk + u * sub, sub)
            v = vt_ref[hh * HEAD_DIM:(hh + 1) * HEAD_DIM, pl.ds(start, sub)]
            v1 = jnp.concatenate([v, ones_rows], axis=0)
            acc_ref[hh] = alpha * acc_ref[hh] + jnp.dot(v1, e, preferred_element_type=F32)
            state[hh] = m_new
        return tuple(state)

    ones_rows = jnp.ones((DENOM_ROWS, sub), BF16)
    acc_ref[...] = jnp.zeros_like(acc_ref)
    init = jnp.full((1, tq), -jnp.inf, F32)
    n_full = qi * (tq // tk)
    n_diag = tq // tk
    for i in range(cross):
        scores(0, i, True)
    n_pairs = lax.shift_right_logical(n_full, 1)
    carry = lax.fori_loop(
        0, n_pairs,
        lambda i, c: trip([(2 * i, False), (2 * i + 1, False)], c, (2 * i + 2, True)), (init,) * heads)
    carry = lax.fori_loop(2 * n_pairs, n_full, lambda j, c: trip([(j, False)], c, (j + 1, True)), carry)
    for d in range(n_diag):
        carry = trip([(n_full + d, True)], carry, (n_full + d + 1, True) if d + 1 < n_diag else None)
    for hh in range(heads):
        acc = acc_ref[hh]
        ot_ref[hh * HEAD_DIM:(hh + 1) * HEAD_DIM, :] = acc[:HEAD_DIM] / acc[HEAD_DIM:HEAD_DIM + 1]


def _fox_attn(qt, k, kf, vt, ft, B, tq=ATTN_Q, tk=ATTN_K, sub=ATTN_SUB, ahead=ATTN_AHEAD, cross=ATTN_CROSS,
              q_blocks=ATTN_Q_BLOCKS):
    D, N = qt.shape
    T = N // B
    ng = T // (tq * q_blocks)
    qblk = pl.BlockSpec((LANE, tq * q_blocks), lambda b, p, i: (p, b * ng + i))
    keys = pl.BlockSpec((T, LANE), lambda b, p, i: (b, p))
    return pl.pallas_call(
        functools.partial(_fox_attn_kernel, tq, tk, sub, ahead, cross, q_blocks),
        out_shape=jax.ShapeDtypeStruct((D, N), F32),
        grid=(B, D // LANE, ng),
        in_specs=[qblk, keys, keys,
                  pl.BlockSpec((LANE, T), lambda b, p, i: (p, b)),
                  pl.BlockSpec((1, ft.shape[1], tq * q_blocks), lambda b, p, i: (b, 0, i))],
        out_specs=qblk,
        scratch_shapes=[pltpu.VMEM((ahead + 1, sub, tq), F32),
                        pltpu.VMEM((LANE // HEAD_DIM, HEAD_DIM + DENOM_ROWS, tq), F32)],
        compiler_params=_params(("parallel", "parallel", "arbitrary")),
        name="fox_attn",
    )(qt, k, kf, vt, ft)


def kernel(x, c, mod_w, mod_b, norm_mix_g, norm_mlp_g, mlp_up, mlp_down, rw_mu, rw_wr, rw_wk, rw_wv, rw_wo, rw_w0, rw_w1, rw_w2, rw_a0, rw_a1, rw_a2, rw_g1, rw_g2, rw_kk, rw_ka, rw_rk, rw_lnw, rw_lnb, rw_v0, rw_v1, rw_v2, kv_norm_g, kv_mod_w, kv_mod_b, kv_w, kv_fb, kv_kg, fx_wqg, fx_qg, fx_wo, final_g):
    B, T, D = x.shape
    depth = mod_w.shape[0]
    n_a = rw_wr.shape[0]
    H = D // HEAD_DIM
    N = B * T
    assert D % LANE == 0 and B <= SUBLANE and H <= LANE
    assert T % TOKEN_TILE == 0 and T % WKV_SCAN_TOKENS == 0 and N % WKV_PREP_TOKENS == 0
    assert T % (ATTN_Q * ATTN_Q_BLOCKS) == 0 and ATTN_Q % ATTN_K == 0 and ATTN_K % ATTN_SUB == 0
    bf = lambda w: w.astype(BF16)
    row = lambda vct: vct.reshape(1, -1)

    c_pad = jnp.pad(c, ((0, SUBLANE - B), (0, 0)))
    mods = _modulation(c_pad, mod_w, mod_b)[:, :B].reshape(depth, B, N_MOD, 1, D)
    kv_mod = _modulation(c_pad, kv_mod_w[None], kv_mod_b[None])[0, :B].reshape(B, 2, 1, D)

    head_of = jnp.arange(D, dtype=jnp.int32) // HEAD_DIM
    ind = (head_of[:, None] == jnp.arange(LANE, dtype=jnp.int32)[None, :]).astype(F32)
    indt = ind.T

    xf = x.reshape(N, D)
    v_first = None
    k_sh = kf_sh = vt_sh = ft_sh = None
    for i in range(depth):
        if i < n_a:
            p = dict(mu=rw_mu[i], wr=bf(rw_wr[i]), wk=bf(rw_wk[i]), wv=bf(rw_wv[i]),
                     w0=row(rw_w0[i]), w1=bf(rw_w1[i]), w2=bf(rw_w2[i]),
                     a0=row(rw_a0[i]), a1=bf(rw_a1[i]), a2=bf(rw_a2[i]),
                     g1=bf(rw_g1[i]), g2=bf(rw_g2[i]), kk=row(rw_kk[i]), ka=row(rw_ka[i]))
            if i > 0:
                p.update(v0=row(rw_v0[i - 1]), v1=bf(rw_v1[i - 1]), v2=bf(rw_v2[i - 1]))
            r, k, v, lw, kk, a, gate = _rwkv_pre(xf, mods[i], row(norm_mix_g[i]), p,
                                                 v_first if i > 0 else None, T)
            if i == 0:
                v_first = v
            m, n, rp, yl, bonus = _wkv_prep(r, k, v, lw, kk, a, row(rw_rk[i]))
            y = _wkv_scan(m, n, rp, yl, bonus, row(rw_lnw[i]), row(rw_lnb[i]), B)
            gmul, wo = gate, bf(rw_wo[i])
        else:
            j = i - n_a
            if j == 0:
                wf = jnp.pad(kv_w[:, 2 * D:], ((0, 0), (0, LANE - H)))
                fb = jnp.pad(kv_fb, (0, LANE - H)).reshape(1, LANE)
                k_sh, vt_sh, f_sh, kf_sh = _shared_kv(xf, kv_mod, row(kv_norm_g), bf(kv_w[:, :D]),
                                                      bf(kv_w[:, D:2 * D].T), bf(wf), fb,
                                                      row(jnp.tile(kv_kg, H)), ind, indt, T)
                ft_sh = f_sh[:, :H].reshape(B, T, H).transpose(0, 2, 1)
            qg_rep = jnp.broadcast_to(jnp.tile(fx_qg[j], H)[:, None], (D, TOKEN_TILE))
            qt, gmul = _fox_q(xf, mods[i], row(norm_mix_g[i]), bf(fx_wqg[j][:, :D].T), bf(fx_wqg[j][:, D:].T),
                              qg_rep, T)
            y = _fox_attn(qt, k_sh, kf_sh, vt_sh, ft_sh, B)
            wo = bf(fx_wo[j])
        xf = _post_mlp(xf, y, gmul, mods[i], wo, row(norm_mlp_g[i]), bf(mlp_up[i]), bf(mlp_down[i]), T, i >= n_a,
                       final_g=row(final_g) if i == depth - 1 else None)
    return xf.reshape(B, T, D)
```

```python
import functools

import jax
import jax.numpy as jnp
from jax import lax
from jax.experimental import pallas as pl
from jax.experimental.pallas import tpu as pltpu

F32 = jnp.float32
BF16 = jnp.bfloat16

HEAD_DIM = 64
N_MOD = 6
NORM_EPS = 1e-6
GN_EPS = 64e-5
L2_EPS = 1e-12
WKV_CHUNK = 64
LANE = 128
SUBLANE = 8
VMEM_LIMIT = 56 * 1024 * 1024

TOKEN_TILE = 256
MOD_COLS = 1024
WKV_PREP_TOKENS = 1024
WKV_SCAN_TOKENS = 128
ATTN_Q = 512
ATTN_K = 512
ATTN_SUB = 256
ATTN_AHEAD = 3
ATTN_CROSS = 2
ATTN_Q_BLOCKS = 4
LOG2E = 1.4426950408889634
DENOM_ROWS = 16
F_PIECES = 3


def _bdot(a, b):
    return jnp.dot(a.astype(BF16), b.astype(BF16), preferred_element_type=F32)


def _bdot_nt(a, b):
    return lax.dot_general(a.astype(BF16), b.astype(BF16), (((1,), (1,)), ((), ())),
                           preferred_element_type=F32)


def _bdot_tn(a, b):
    return lax.dot_general(a.astype(BF16), b.astype(BF16), (((0,), (0,)), ((), ())),
                           preferred_element_type=F32)


def _split2(x):
    hi = x.astype(BF16)
    return hi, (x - hi.astype(F32)).astype(BF16)


def _split3(x):
    hi = x.astype(BF16)
    rest = x - hi.astype(F32)
    mid = rest.astype(BF16)
    lo = (rest - mid.astype(F32)).astype(BF16)
    return hi, mid, lo


def _dot_sel_lhs(sel, x):
    n = x.shape[1]
    y = jnp.dot(sel.astype(BF16), jnp.concatenate(_split3(x), axis=1), preferred_element_type=F32)
    return y[:, :n] + y[:, n:2 * n] + y[:, 2 * n:]


def _dot_sel_rhs(x, sel):
    m = x.shape[0]
    y = jnp.dot(jnp.concatenate(_split3(x), axis=0), sel.astype(BF16), preferred_element_type=F32)
    return y[:m] + y[m:2 * m] + y[2 * m:]


def _rms_mod(x, g, shift, scale):
    ms = jnp.mean(x * x, axis=-1, keepdims=True)
    y = x * lax.rsqrt(ms + NORM_EPS) * g
    return y * (1.0 + scale) + shift


def _const_spec(shape):
    n = len(shape)
    return pl.BlockSpec(shape, lambda *_: (0,) * n, pipeline_mode=pl.Buffered(1))


def _params(sem):
    return pltpu.CompilerParams(dimension_semantics=sem, vmem_limit_bytes=VMEM_LIMIT)


def _mod_kernel(c_ref, w_ref, b_ref, o_ref):
    c = c_ref[...]
    ca = c * jax.nn.sigmoid(c)
    rows = jnp.concatenate(_split3(ca), axis=0)
    w_hi, w_lo = _split2(w_ref[0])
    y = jnp.dot(rows, w_hi, preferred_element_type=F32)
    y_lo = jnp.dot(rows[:2 * SUBLANE], w_lo, preferred_element_type=F32)
    o_ref[0] = (y[:SUBLANE] + y[SUBLANE:2 * SUBLANE] + y[2 * SUBLANE:]
                + y_lo[:SUBLANE] + y_lo[SUBLANE:] + b_ref[0])


def _modulation(c_pad, w, b, tn=MOD_COLS):
    L, D, M = w.shape
    return pl.pallas_call(
        _mod_kernel,
        out_shape=jax.ShapeDtypeStruct((L, SUBLANE, M), F32),
        grid=(L, M // tn),
        in_specs=[pl.BlockSpec((SUBLANE, D), lambda l, j: (0, 0)),
                  pl.BlockSpec((1, D, tn), lambda l, j: (l, 0, j)),
                  pl.BlockSpec((1, 1, tn), lambda l, j: (l, 0, j))],
        out_specs=pl.BlockSpec((1, SUBLANE, tn), lambda l, j: (l, 0, j)),
        compiler_params=_params(("parallel", "parallel")),
        name="modulation",
    )(c_pad, w, b.reshape(L, 1, M))


def _rwkv_pre_kernel(has_vres, tiles_per_batch, *refs):
    if has_vres:
        (x_ref, xp_ref, mod_ref, g_ref, mu_ref, wr_ref, wk_ref, wv_ref, w0_ref, w1_ref, w2_ref,
         a0_ref, a1_ref, a2_ref, g1_ref, g2_ref, kk_ref, ka_ref,
         vf_ref, v0_ref, v1_ref, v2_ref,
         r_out, k_out, v_out, lw_out, kk_out, a_out, g_out) = refs
    else:
        (x_ref, xp_ref, mod_ref, g_ref, mu_ref, wr_ref, wk_ref, wv_ref, w0_ref, w1_ref, w2_ref,
         a0_ref, a1_ref, a2_ref, g1_ref, g2_ref, kk_ref, ka_ref,
         r_out, k_out, v_out, lw_out, kk_out, a_out, g_out) = refs
    i = pl.program_id(0)
    shift, scale = mod_ref[0, 0], mod_ref[0, 1]
    g = g_ref[...]
    h = _rms_mod(x_ref[...], g, shift, scale)
    tm = h.shape[0]
    h_last = _rms_mod(xp_ref[SUBLANE - 1:SUBLANE, :], g, shift, scale)
    h_last = jnp.where(i % tiles_per_batch == 0, 0.0, h_last)
    row = lax.broadcasted_iota(jnp.int32, (tm, 1), 0)
    h_prev = jnp.where(row == 0, h_last, pltpu.roll(h, 1, axis=0))
    xx = h_prev - h
    xr, xw, xk, xv, xa, xg = (h + xx * mu_ref[j:j + 1, :] for j in range(6))

    r = _bdot(xr, wr_ref[...])
    k = _bdot(xk, wk_ref[...])
    v = _bdot(xv, wv_ref[...])
    w_log = -jax.nn.softplus(-(w0_ref[...] + _bdot(jnp.tanh(_bdot(xw, w1_ref[...])), w2_ref[...]))) - 0.5
    if has_vres:
        mix = jax.nn.sigmoid(v0_ref[...] + _bdot(_bdot(xv, v1_ref[...]), v2_ref[...]))
        v = v + (vf_ref[...] - v) * mix
    a = jax.nn.sigmoid(a0_ref[...] + _bdot(_bdot(xa, a1_ref[...]), a2_ref[...]))
    gate = _bdot(jax.nn.sigmoid(_bdot(xg, g1_ref[...])), g2_ref[...])

    r_out[...] = r
    k_out[...] = k * (1.0 + (a - 1.0) * ka_ref[...])
    v_out[...] = v
    lw_out[...] = -jnp.exp(w_log)
    kk_out[...] = k * kk_ref[...]
    a_out[...] = a
    g_out[...] = gate


def _rwkv_pre(x, mod, norm_g, p, v_first, T, tm=TOKEN_TILE):
    N, D = x.shape
    tpb = T // tm
    has_vres = v_first is not None
    tile = pl.BlockSpec((tm, D), lambda i: (i, 0))
    prev = pl.BlockSpec((SUBLANE, D), lambda i: (jnp.maximum(i * (tm // SUBLANE) - 1, 0), 0))
    mods = pl.BlockSpec((1, N_MOD, 1, D), lambda i: (i // tpb, 0, 0, 0))
    vec = _const_spec((1, D))
    args = [x, x, mod, norm_g, p["mu"], p["wr"], p["wk"], p["wv"], p["w0"], p["w1"], p["w2"],
            p["a0"], p["a1"], p["a2"], p["g1"], p["g2"], p["kk"], p["ka"]]
    specs = [tile, prev, mods, vec, _const_spec(p["mu"].shape)]
    specs += [_const_spec(a.shape) for a in args[5:]]
    if has_vres:
        args += [v_first, p["v0"], p["v1"], p["v2"]]
        specs += [tile, vec, _const_spec(p["v1"].shape), _const_spec(p["v2"].shape)]
    out = jax.ShapeDtypeStruct((N, D), F32)
    return pl.pallas_call(
        functools.partial(_rwkv_pre_kernel, has_vres, tpb),
        out_shape=(out,) * 7,
        grid=(N // tm,),
        in_specs=specs,
        out_specs=(tile,) * 7,
        compiler_params=_params(("parallel",)),
        name="rwkv_pre",
    )(*args)


def _blockdiag(w, diag):
    return jnp.where(diag, jnp.concatenate([w, w], axis=0), jnp.zeros((), w.dtype))


def _pair_mm(l, rs, diag):
    rhs = jnp.concatenate([_blockdiag(r.astype(BF16), diag) for r in rs], axis=1)
    return jnp.dot(l.astype(BF16), rhs, preferred_element_type=F32)


def _pair_mm_nt(l, rs, diag):
    rhs = jnp.concatenate([_blockdiag(r.astype(BF16), diag) for r in rs], axis=0)
    return lax.dot_general(l.astype(BF16), rhs, (((1,), (1,)), ((), ())), preferred_element_type=F32)


def _pair_mm_tn(l, rs, first):
    rhs = jnp.concatenate([r.astype(BF16) for r in rs], axis=1)
    full = lax.dot_general(l.astype(BF16), rhs, (((0,), (0,)), ((), ())), preferred_element_type=F32)
    return [jnp.where(first, full[:HEAD_DIM, i * LANE:(i + 1) * LANE], full[HEAD_DIM:, i * LANE:(i + 1) * LANE])
            for i in range(len(rs))]


def _pair_masks(C):
    row = lax.broadcasted_iota(jnp.int32, (C, LANE), 0)
    lane = lax.broadcasted_iota(jnp.int32, (C, LANE), 1)
    col = lane & (HEAD_DIM - 1)
    first = lane < HEAD_DIM
    row2 = lax.broadcasted_iota(jnp.int32, (LANE, LANE), 0)
    lane2 = lax.broadcasted_iota(jnp.int32, (LANE, LANE), 1)
    diag = (row2 < HEAD_DIM) == (lane2 < HEAD_DIM)
    return row, col, first, diag


def _pair_sum(x, first):
    s0 = jnp.sum(jnp.where(first, x, 0.0), axis=-1, keepdims=True)
    s1 = jnp.sum(jnp.where(first, 0.0, x), axis=-1, keepdims=True)
    return jnp.where(first, s0, s1)


def _wkv_prep_kernel(r_ref, k_ref, v_ref, lw_ref, kk_ref, a_ref, rk_ref,
                     m_out, n_out, rp_out, yl_out, bonus_out):
    C = WKV_CHUNK
    chunks = [slice(ci * C, (ci + 1) * C) for ci in range(r_ref.shape[0] // C)]
    row, col, first, diag = _pair_masks(C)
    tri_incl, tri_strict, eye = row >= col, row > col, row == col
    cum_op = (lax.broadcasted_iota(jnp.int32, (C, C), 0) >= lax.broadcasted_iota(jnp.int32, (C, C), 1)).astype(F32)

    def same_block(s):
        return (row ^ col) < s
    levels, s = [], 8
    while s < C:
        levels.append((same_block(s), same_block(2 * s)))
        s *= 2

    at, rt, bt, kt, bh, kh, w_end, vs = [], [], [], [], [], [], [], []
    for rows in chunks:
        lw, kkr = lw_ref[rows, :] * LOG2E, kk_ref[rows, :]
        kk = kkr / jnp.maximum(jnp.sqrt(_pair_sum(kkr * kkr, first)), L2_EPS)
        bvec = kk * a_ref[rows, :]
        cw = _dot_sel_lhs(cum_op, lw)
        cw_end = cw[C - 1:C, :]
        w_inv, w_rest = jnp.exp2(-cw), jnp.exp2(cw_end - cw)
        k = k_ref[rows, :]
        at.append(-kk * jnp.exp2(cw - lw))
        rt.append(r_ref[rows, :] * jnp.exp2(cw))
        bt.append(bvec * w_inv)
        kt.append(k * w_inv)
        bh.append(bvec * w_rest)
        kh.append(k * w_rest)
        w_end.append(jnp.exp2(cw_end))
        vs.append(v_ref[rows, :])
    n = range(len(chunks))

    amat = [_pair_mm_nt(jnp.concatenate([at[i], rt[i]], axis=0), [bt[i], kt[i]], diag) for i in n]
    a_ab = [jnp.where(tri_strict, amat[i][:C, :LANE], 0.0) for i in n]
    a_ak = [jnp.where(tri_strict, amat[i][:C, LANE:], 0.0) for i in n]
    a_rb = [jnp.where(tri_incl, amat[i][C:, :LANE], 0.0) for i in n]
    a_rk = [jnp.where(tri_incl, amat[i][C:, LANE:], 0.0) for i in n]
    akv = [_pair_mm(a_ak[i], [vs[i]], diag) for i in n]

    blk8 = same_block(8)
    x = [jnp.where(blk8, a_ab[i], 0.0) for i in n]
    inv = [jnp.where(eye, 1.0, x[i]) for i in n]
    for _ in range(2):
        x = [_pair_mm(x[i], [x[i]], diag) for i in n]
        inv = [inv[i] + _pair_mm(x[i], [inv[i]], diag) for i in n]
    for inner, outer in levels:
        off = [jnp.where(outer, jnp.where(inner, 0.0, a_ab[i]), 0.0) for i in n]
        t = [_pair_mm(off[i], [inv[i]], diag) for i in n]
        inv = [inv[i] + _pair_mm(inv[i], [t[i]], diag) for i in n]

    sol = [_pair_mm(inv[i], [at[i], akv[i]], diag) for i in n]
    ap = [sol[i][:, :LANE] for i in n]
    ul = [sol[i][:, LANE:] for i in n]
    for i in n:
        rows = chunks[i]
        rb = _pair_mm(a_rb[i], [ap[i], ul[i]], diag)
        rp_out[rows, :] = rt[i] + rb[:, :LANE]
        yl_out[rows, :] = rb[:, LANE:] + _pair_mm(a_rk[i], [vs[i]], diag)
        bh_ap, bh_ul = _pair_mm_tn(bh[i], [ap[i], ul[i]], first)
        m_out[rows, :] = jnp.where(eye, w_end[i], 0.0) + bh_ap
        n_out[rows, :] = bh_ul + _pair_mm_tn(kh[i], [vs[i]], first)[0]
        bonus_out[rows, :] = _pair_sum(r_ref[rows, :] * k_ref[rows, :] * rk_ref[...], first) * vs[i]


def _wkv_prep(r, k, v, lw, kk, a, rk, tb=WKV_PREP_TOKENS):
    N, D = r.shape
    assert WKV_CHUNK == HEAD_DIM and 2 * HEAD_DIM == LANE
    blk = pl.BlockSpec((tb, LANE), lambda i, p: (i, p))
    out = jax.ShapeDtypeStruct((N, D), F32)
    return pl.pallas_call(
        _wkv_prep_kernel,
        out_shape=(out,) * 5,
        grid=(N // tb, D // LANE),
        in_specs=[blk] * 6 + [pl.BlockSpec((1, LANE), lambda i, p: (0, p))],
        out_specs=(blk,) * 5,
        compiler_params=_params(("parallel", "parallel")),
        name="wkv_prep",
    )(r, k, v, lw, kk, a, rk)


def _wkv_scan_kernel(m_ref, n_ref, rp_ref, yl_ref, bonus_ref, lnw_ref, lnb_ref, y_out, s_ref):
    C = WKV_CHUNK
    B, tb, D = m_ref.shape
    _, _, _, diag = _pair_masks(C)
    head_ones = diag.astype(F32)

    @pl.when(pl.program_id(0) == 0)
    def _():
        s_ref[...] = jnp.zeros_like(s_ref)

    pairs = [(b, slice(p * LANE, (p + 1) * LANE)) for b in range(B) for p in range(D // LANE)]
    for ci in range(tb // C):
        rows = slice(ci * C, (ci + 1) * C)
        prods = []
        for b, lanes in pairs:
            s_hi, s_lo = _split2(s_ref[b, :, lanes])
            d_hi, d_lo = _blockdiag(s_hi, diag), _blockdiag(s_lo, diag)
            a_hi, a_lo = _split2(jnp.concatenate([rp_ref[b, rows, lanes], m_ref[b, rows, lanes]], axis=0))
            prods.append(jnp.dot(jnp.concatenate([a_hi, a_hi, a_lo], axis=1),
                                 jnp.concatenate([d_hi, d_lo, d_hi], axis=0), preferred_element_type=F32))
        ys = [pr[:C] + yl_ref[b, rows, lanes] for (b, lanes), pr in zip(pairs, prods)]
        for (b, lanes), pr in zip(pairs, prods):
            s_ref[b, :, lanes] = pr[C:] + n_ref[b, rows, lanes]
        ycs = [y - _bdot(y, head_ones) * (1.0 / HEAD_DIM) for y in ys]
        vrs = [_bdot(yc * yc, head_ones) * (1.0 / HEAD_DIM) for yc in ycs]
        for (b, lanes), yc, var in zip(pairs, ycs, vrs):
            yn = yc * lax.rsqrt(var + GN_EPS) * lnw_ref[:, lanes] + lnb_ref[:, lanes]
            y_out[b, rows, lanes] = yn + bonus_ref[b, rows, lanes]


def _wkv_scan(m, n, rp, yl, bonus, lnw, lnb, B, tb=WKV_SCAN_TOKENS):
    N, D = m.shape
    T = N // B
    blk = pl.BlockSpec((B, tb, D), lambda t: (0, t, 0))
    vec = pl.BlockSpec((1, D), lambda t: (0, 0))
    args = [a.reshape(B, T, D) for a in (m, n, rp, yl, bonus)]
    return pl.pallas_call(
        _wkv_scan_kernel,
        out_shape=jax.ShapeDtypeStruct((B, T, D), F32),
        grid=(T // tb,),
        in_specs=[blk] * 5 + [vec] * 2,
        out_specs=blk,
        scratch_shapes=[pltpu.VMEM((B, HEAD_DIM, D), F32)],
        compiler_params=_params(("arbitrary",)),
        name="wkv_scan",
    )(*args, lnw, lnb).reshape(N, D)


def _post_mlp_kernel(final, channel_major, *refs):
    if final:
        (x_ref, y_ref, gm_ref, mod_ref, wo_ref, g_ref, up_ref, down_ref, fg_ref, o_ref) = refs
    else:
        (x_ref, y_ref, gm_ref, mod_ref, wo_ref, g_ref, up_ref, down_ref, o_ref) = refs
    gt1, sh2, sc2, gt2 = mod_ref[0, 2], mod_ref[0, 3], mod_ref[0, 4], mod_ref[0, 5]
    gated = y_ref[...] * gm_ref[...]
    if channel_major:
        mix = _bdot_tn(gated, wo_ref[...])
    else:
        mix = _bdot(gated, wo_ref[...])
    x = x_ref[...] + (1.0 + gt1) * mix
    h = _rms_mod(x, g_ref[...], sh2, sc2)
    u = jnp.maximum(_bdot(h, up_ref[...]), 0.0)
    x = x + (1.0 + gt2) * _bdot(u * u, down_ref[...])
    if final:
        ms = jnp.mean(x * x, axis=-1, keepdims=True)
        x = x * lax.rsqrt(ms + NORM_EPS) * fg_ref[...]
    o_ref[...] = x


def _post_mlp(x, y, gmul, mod, wo, norm_g, w_up, w_down, T, channel_major, final_g=None, tm=TOKEN_TILE):
    N, D = x.shape
    tpb = T // tm
    final = final_g is not None
    tile = pl.BlockSpec((tm, D), lambda i: (i, 0))
    mixer = pl.BlockSpec((D, tm), lambda i: (0, i)) if channel_major else tile
    mods = pl.BlockSpec((1, N_MOD, 1, D), lambda i: (i // tpb, 0, 0, 0))
    args = [x, y, gmul, mod, wo, norm_g, w_up, w_down]
    specs = [tile, mixer, mixer, mods, _const_spec(wo.shape), _const_spec((1, D)),
             _const_spec(w_up.shape), _const_spec(w_down.shape)]
    if final:
        args.append(final_g)
        specs.append(_const_spec((1, D)))
    return pl.pallas_call(
        functools.partial(_post_mlp_kernel, final, channel_major),
        out_shape=jax.ShapeDtypeStruct((N, D), F32),
        grid=(N // tm,),
        in_specs=specs,
        out_specs=tile,
        compiler_params=_params(("parallel",)),
        name="post_mlp",
    )(*args)


def _shared_kv_kernel(tiles_per_batch, x_ref, mod_ref, g_ref, wk_ref, wvt_ref, wf_ref, fb_ref, kg_ref,
                      ind_ref, indt_ref, place_ref, k_out, vt_out, f_out, kf_out, carry_ref):
    i = pl.program_id(0)

    @pl.when(i % tiles_per_batch == 0)
    def _():
        carry_ref[...] = jnp.zeros_like(carry_ref)

    h = _rms_mod(x_ref[...], g_ref[...], mod_ref[0, 0], mod_ref[0, 1])
    tm = h.shape[0]
    k = _bdot(h, wk_ref[...])
    ss = _dot_sel_rhs(k * k, ind_ref[...])
    ms = _dot_sel_rhs(ss, indt_ref[...]) * (1.0 / HEAD_DIM)
    k_out[...] = (k * lax.rsqrt(ms + NORM_EPS) * kg_ref[...]).astype(BF16)
    vt_out[...] = _bdot_nt(wvt_ref[...], h).astype(BF16)
    log_f = jax.nn.log_sigmoid(_bdot(h, wf_ref[...]) + fb_ref[...])
    row = lax.broadcasted_iota(jnp.int32, (tm, tm), 0)
    col = lax.broadcasted_iota(jnp.int32, (tm, tm), 1)
    cum = _dot_sel_lhs((row >= col).astype(F32), log_f) + carry_ref[...]
    f_out[...] = cum
    carry_ref[...] = cum[tm - 1:tm, :]
    pieces = jnp.concatenate(_split3(cum * LOG2E), axis=1)
    kf_out[...] = jnp.dot(pieces, place_ref[...], preferred_element_type=F32).astype(BF16)


def _bias_placement(H, D):
    heads = LANE // HEAD_DIM
    r = jnp.arange(F_PIECES * LANE, dtype=jnp.int32)
    piece, head = r // LANE, r % LANE
    target = (head // heads) * LANE + (head % heads) * F_PIECES + piece
    hit = (target[:, None] == jnp.arange(D, dtype=jnp.int32)[None, :]) & (head < H)[:, None]
    return hit.astype(BF16)


def _shared_kv(x, mod, norm_g, wk, wvt, wf, fb, kg, ind, indt, T, tm=TOKEN_TILE):
    N, D = x.shape
    tpb = T // tm
    place = _bias_placement(D // HEAD_DIM, D)
    tile = pl.BlockSpec((tm, D), lambda i: (i, 0))
    mods = pl.BlockSpec((1, 2, 1, D), lambda i: (i // tpb, 0, 0, 0))
    return pl.pallas_call(
        functools.partial(_shared_kv_kernel, tpb),
        out_shape=(jax.ShapeDtypeStruct((N, D), BF16), jax.ShapeDtypeStruct((D, N), BF16),
                   jax.ShapeDtypeStruct((N, LANE), F32), jax.ShapeDtypeStruct((N, D), BF16)),
        grid=(N // tm,),
        in_specs=[tile, mods, _const_spec((1, D)), _const_spec(wk.shape), _const_spec(wvt.shape),
                  _const_spec(wf.shape), _const_spec((1, LANE)), _const_spec((1, D)),
                  _const_spec(ind.shape), _const_spec(indt.shape), _const_spec(place.shape)],
        out_specs=(tile, pl.BlockSpec((D, tm), lambda i: (0, i)), pl.BlockSpec((tm, LANE), lambda i: (i, 0)),
                   tile),
        scratch_shapes=[pltpu.VMEM((1, LANE), F32)],
        compiler_params=_params(("arbitrary",)),
        name="shared_kv",
    )(x, mod, norm_g, wk, wvt, wf, fb, kg, ind, indt, place)


def _fox_q_kernel(x_ref, mod_ref, g_ref, wqt_ref, wgt_ref, qg_ref, qt_out, sgt_out):
    h = _rms_mod(x_ref[...], g_ref[...], mod_ref[0, 0], mod_ref[0, 1])
    tm = h.shape[0]
    qt = _bdot_nt(wqt_ref[...], h)
    q3 = qt.reshape(qt.shape[0] // HEAD_DIM, HEAD_DIM, tm)
    q3 = q3 * lax.rsqrt(jnp.mean(q3 * q3, axis=1, keepdims=True) + NORM_EPS)
    qt_out[...] = (q3.reshape(qt.shape) * qg_ref[...] * (HEAD_DIM ** -0.5 * LOG2E)).astype(BF16)
    sgt_out[...] = jax.nn.sigmoid(_bdot_nt(wgt_ref[...], h))


def _fox_q(x, mod, norm_g, wqt, wgt, qg_rep, T):
    N, D = x.shape
    tm = qg_rep.shape[1]
    tpb = T // tm
    tile = pl.BlockSpec((tm, D), lambda i: (i, 0))
    ttile = pl.BlockSpec((D, tm), lambda i: (0, i))
    mods = pl.BlockSpec((1, N_MOD, 1, D), lambda i: (i // tpb, 0, 0, 0))
    return pl.pallas_call(
        _fox_q_kernel,
        out_shape=(jax.ShapeDtypeStruct((D, N), BF16), jax.ShapeDtypeStruct((D, N), F32)),
        grid=(N // tm,),
        in_specs=[tile, mods, _const_spec((1, D)), _const_spec(wqt.shape), _const_spec(wgt.shape),
                  _const_spec((D, tm))],
        out_specs=(ttile, ttile),
        compiler_params=_params(("parallel",)),
        name="fox_q",
    )(x, mod, norm_g, wqt, wgt, qg_rep)


def _fox_attn_kernel(tq, tk, sub, ahead, cross, q_blocks, qt_ref, k_ref, kf_ref, vt_ref, ft_ref, ot_ref,
                     z_ref, acc_ref):
    for blk in range(q_blocks):
        lanes = slice(blk * tq, (blk + 1) * tq)
        _fox_attn_block(tq, tk, sub, ahead, cross, pl.program_id(2) * q_blocks + blk,
                        qt_ref.at[:, lanes], k_ref, kf_ref, vt_ref, ft_ref.at[:, :, lanes],
                        ot_ref.at[:, lanes], z_ref, acc_ref)


def _fox_attn_block(tq, tk, sub, ahead, cross, qi, qt_ref, k_ref, kf_ref, vt_ref, ft_ref, ot_ref, z_ref, acc_ref):
    p = pl.program_id(1)
    heads = LANE // HEAD_DIM

    chan = lax.broadcasted_iota(jnp.int32, (2 * LANE, tq), 0)
    key_in_sub = lax.broadcasted_iota(jnp.int32, (sub, tq), 0)
    query_in_blk = lax.broadcasted_iota(jnp.int32, (sub, tq), 1)
    q = qt_ref[...]
    q2 = jnp.concatenate([q, jnp.zeros_like(q)], axis=0)
    qz = []
    for hh in range(heads):
        own = (chan >= hh * HEAD_DIM) & (chan < (hh + 1) * HEAD_DIM)
        bias = (chan >= LANE + hh * F_PIECES) & (chan < LANE + (hh + 1) * F_PIECES)
        qz.append(jnp.where(own, q2, jnp.where(bias, -1.0, 0.0).astype(q.dtype)))
    f_t = [ft_ref[0, pl.ds(p * heads + hh, 1), :] * LOG2E for hh in range(heads)]
    steps = [(u, hh) for u in range(tk // sub) for hh in range(heads)]

    n_steps, n_slots = len(steps), ahead + 1
    assert n_steps % n_slots == 0 and cross <= ahead <= n_steps

    def scores(j, idx, masked):
        u, hh = steps[idx]
        start = pl.multiple_of(j * tk + u * sub, sub)
        keys = jnp.concatenate([k_ref[pl.ds(start, sub), :], kf_ref[pl.ds(start, sub), :]], axis=1)
        z = jnp.dot(keys, qz[hh], preferred_element_type=F32)
        if masked:
            z = jnp.where(start + key_in_sub <= qi * tq + query_in_blk, z, -jnp.inf)
        z_ref[idx % n_slots] = z

    def trip(tiles, carry, next_tile):
        seq = [(j, idx, masked) for j, masked in tiles for idx in range(n_steps)]
        n_own = len(seq)
        if next_tile is not None:
            seq += [(next_tile[0], idx, next_tile[1]) for idx in range(cross)]
        for g in range(cross, min(ahead, len(seq))):
            scores(*seq[g])
        state = list(carry)
        for g in range(n_own):
            if g + ahead < len(seq):
                scores(*seq[g + ahead])
            j, idx, _ = seq[g]
            u, hh = steps[idx]
            z = z_ref[idx % n_slots]
            m = state[hh]
            m_new = jnp.maximum(m, jnp.max(z, axis=0, keepdims=True) + f_t[hh])
            alpha = jnp.exp2(m - m_new)
            e = jnp.exp2((z - (m_new - f_t[hh])).astype(BF16))
            start = pl.multiple_of(j * tk + u * sub, sub)
            v = vt_ref[hh * HEAD_DIM:(hh + 1) * HEAD_DIM, pl.ds(start, sub)]
            v1 = jnp.concatenate([v, ones_rows], axis=0)
            acc_ref[hh] = alpha * acc_ref[hh] + jnp.dot(v1, e, preferred_element_type=F32)
            state[hh] = m_new
        return tuple(state)

    ones_rows = jnp.ones((DENOM_ROWS, sub), BF16)
    acc_ref[...] = jnp.zeros_like(acc_ref)
    init = jnp.full((1, tq), -jnp.inf, F32)
    n_full = qi * (tq // tk)
    n_diag = tq // tk
    for i in range(cross):
        scores(0, i, True)
    n_pairs = lax.shift_right_logical(n_full, 1)
    carry = lax.fori_loop(
        0, n_pairs,
        lambda i, c: trip([(2 * i, False), (2 * i + 1, False)], c, (2 * i + 2, True)), (init,) * heads)
    carry = lax.fori_loop(2 * n_pairs, n_full, lambda j, c: trip([(j, False)], c, (j + 1, True)), carry)
    for d in range(n_diag):
        carry = trip([(n_full + d, True)], carry, (n_full + d + 1, True) if d + 1 < n_diag else None)
    for hh in range(heads):
        acc = acc_ref[hh]
        ot_ref[hh * HEAD_DIM:(hh + 1) * HEAD_DIM, :] = acc[:HEAD_DIM] / acc[HEAD_DIM:HEAD_DIM + 1]


def _fox_attn(qt, k, kf, vt, ft, B, tq=ATTN_Q, tk=ATTN_K, sub=ATTN_SUB, ahead=ATTN_AHEAD, cross=ATTN_CROSS,
              q_blocks=ATTN_Q_BLOCKS):
    D, N = qt.shape
    T = N // B
    ng = T // (tq * q_blocks)
    qblk = pl.BlockSpec((LANE, tq * q_blocks), lambda b, p, i: (p, b * ng + i))
    keys = pl.BlockSpec((T, LANE), lambda b, p, i: (b, p))
    return pl.pallas_call(
        functools.partial(_fox_attn_kernel, tq, tk, sub, ahead, cross, q_blocks),
        out_shape=jax.ShapeDtypeStruct((D, N), F32),
        grid=(B, D // LANE, ng),
        in_specs=[qblk, keys, keys,
                  pl.BlockSpec((LANE, T), lambda b, p, i: (p, b)),
                  pl.BlockSpec((1, ft.shape[1], tq * q_blocks), lambda b, p, i: (b, 0, i))],
        out_specs=qblk,
        scratch_shapes=[pltpu.VMEM((ahead + 1, sub, tq), F32),
                        pltpu.VMEM((LANE // HEAD_DIM, HEAD_DIM + DENOM_ROWS, tq), F32)],
        compiler_params=_params(("parallel", "parallel", "arbitrary")),
        name="fox_attn",
    )(qt, k, kf, vt, ft)


def kernel(x, c, mod_w, mod_b, norm_mix_g, norm_mlp_g, mlp_up, mlp_down, rw_mu, rw_wr, rw_wk, rw_wv, rw_wo, rw_w0, rw_w1, rw_w2, rw_a0, rw_a1, rw_a2, rw_g1, rw_g2, rw_kk, rw_ka, rw_rk, rw_lnw, rw_lnb, rw_v0, rw_v1, rw_v2, kv_norm_g, kv_mod_w, kv_mod_b, kv_w, kv_fb, kv_kg, fx_wqg, fx_qg, fx_wo, final_g):
    B, T, D = x.shape
    depth = mod_w.shape[0]
    n_a = rw_wr.shape[0]
    H = D // HEAD_DIM
    N = B * T
    assert D % LANE == 0 and B <= SUBLANE and H <= LANE
    assert T % TOKEN_TILE == 0 and T % WKV_SCAN_TOKENS == 0 and N % WKV_PREP_TOKENS == 0
    assert T % (ATTN_Q * ATTN_Q_BLOCKS) == 0 and ATTN_Q % ATTN_K == 0 and ATTN_K % ATTN_SUB == 0
    bf = lambda w: w.astype(BF16)
    row = lambda vct: vct.reshape(1, -1)

    c_pad = jnp.pad(c, ((0, SUBLANE - B), (0, 0)))
    mods = _modulation(c_pad, mod_w, mod_b)[:, :B].reshape(depth, B, N_MOD, 1, D)
    kv_mod = _modulation(c_pad, kv_mod_w[None], kv_mod_b[None])[0, :B].reshape(B, 2, 1, D)

    head_of = jnp.arange(D, dtype=jnp.int32) // HEAD_DIM
    ind = (head_of[:, None] == jnp.arange(LANE, dtype=jnp.int32)[None, :]).astype(F32)
    indt = ind.T

    xf = x.reshape(N, D)
    v_first = None
    k_sh = kf_sh = vt_sh = ft_sh = None
    for i in range(depth):
        if i < n_a:
            p = dict(mu=rw_mu[i], wr=bf(rw_wr[i]), wk=bf(rw_wk[i]), wv=bf(rw_wv[i]),
                     w0=row(rw_w0[i]), w1=bf(rw_w1[i]), w2=bf(rw_w2[i]),
                     a0=row(rw_a0[i]), a1=bf(rw_a1[i]), a2=bf(rw_a2[i]),
                     g1=bf(rw_g1[i]), g2=bf(rw_g2[i]), kk=row(rw_kk[i]), ka=row(rw_ka[i]))
            if i > 0:
                p.update(v0=row(rw_v0[i - 1]), v1=bf(rw_v1[i - 1]), v2=bf(rw_v2[i - 1]))
            r, k, v, lw, kk, a, gate = _rwkv_pre(xf, mods[i], row(norm_mix_g[i]), p,
                                                 v_first if i > 0 else None, T)
            if i == 0:
                v_first = v
            m, n, rp, yl, bonus = _wkv_prep(r, k, v, lw, kk, a, row(rw_rk[i]))
            y = _wkv_scan(m, n, rp, yl, bonus, row(rw_lnw[i]), row(rw_lnb[i]), B)
            gmul, wo = gate, bf(rw_wo[i])
        else:
            j = i - n_a
            if j == 0:
                wf = jnp.pad(kv_w[:, 2 * D:], ((0, 0), (0, LANE - H)))
                fb = jnp.pad(kv_fb, (0, LANE - H)).reshape(1, LANE)
                k_sh, vt_sh, f_sh, kf_sh = _shared_kv(xf, kv_mod, row(kv_norm_g), bf(kv_w[:, :D]),
                                                      bf(kv_w[:, D:2 * D].T), bf(wf), fb,
                                                      row(jnp.tile(kv_kg, H)), ind, indt, T)
                ft_sh = f_sh[:, :H].reshape(B, T, H).transpose(0, 2, 1)
            qg_rep = jnp.broadcast_to(jnp.tile(fx_qg[j], H)[:, None], (D, TOKEN_TILE))
            qt, gmul = _fox_q(xf, mods[i], row(norm_mix_g[i]), bf(fx_wqg[j][:, :D].T), bf(fx_wqg[j][:, D:].T),
                              qg_rep, T)
            y = _fox_attn(qt, k_sh, kf_sh, vt_sh, ft_sh, B)
            wo = bf(fx_wo[j])
        xf = _post_mlp(xf, y, gmul, mods[i], wo, row(norm_mlp_g[i]), bf(mlp_up[i]), bf(mlp_down[i]), T, i >= n_a,
                       final_g=row(final_g) if i == depth - 1 else None)
    return xf.reshape(B, T, D)
```

```python
import functools

import jax
import jax.numpy as jnp
from jax import lax
from jax.experimental import pallas as pl
from jax.experimental.pallas import tpu as pltpu

F32 = jnp.float32
BF16 = jnp.bfloat16

HEAD_DIM = 64
N_MOD = 6
NORM_EPS = 1e-6
GN_EPS = 64e-5
L2_EPS = 1e-12
WKV_CHUNK = 64
LANE = 128
SUBLANE = 8
VMEM_LIMIT = 56 * 1024 * 1024

TOKEN_TILE = 256
MOD_COLS = 1024
WKV_PREP_TOKENS = 1024
WKV_SCAN_TOKENS = 128
ATTN_Q = 512
ATTN_K = 512
ATTN_SUB = 256
ATTN_AHEAD = 3
ATTN_CROSS = 2
ATTN_Q_BLOCKS = 4
LOG2E = 1.4426950408889634
DENOM_ROWS = 16
F_PIECES = 3


def _bdot(a, b):
    return jnp.dot(a.astype(BF16), b.astype(BF16), preferred_element_type=F32)


def _bdot_nt(a, b):
    return lax.dot_general(a.astype(BF16), b.astype(BF16), (((1,), (1,)), ((), ())),
                           preferred_element_type=F32)


def _bdot_tn(a, b):
    return lax.dot_general(a.astype(BF16), b.astype(BF16), (((0,), (0,)), ((), ())),
                           preferred_element_type=F32)


def _split2(x):
    hi = x.astype(BF16)
    return hi, (x - hi.astype(F32)).astype(BF16)


def _split3(x):
    hi = x.astype(BF16)
    rest = x - hi.astype(F32)
    mid = rest.astype(BF16)
    lo = (rest - mid.astype(F32)).astype(BF16)
    return hi, mid, lo


def _dot_sel_lhs(sel, x):
    n = x.shape[1]
    y = jnp.dot(sel.astype(BF16), jnp.concatenate(_split3(x), axis=1), preferred_element_type=F32)
    return y[:, :n] + y[:, n:2 * n] + y[:, 2 * n:]


def _dot_sel_rhs(x, sel):
    m = x.shape[0]
    y = jnp.dot(jnp.concatenate(_split3(x), axis=0), sel.astype(BF16), preferred_element_type=F32)
    return y[:m] + y[m:2 * m] + y[2 * m:]


def _rms_mod(x, g, shift, scale):
    ms = jnp.mean(x * x, axis=-1, keepdims=True)
    y = x * lax.rsqrt(ms + NORM_EPS) * g
    return y * (1.0 + scale) + shift


def _const_spec(shape):
    n = len(shape)
    return pl.BlockSpec(shape, lambda *_: (0,) * n, pipeline_mode=pl.Buffered(1))


def _params(sem):
    return pltpu.CompilerParams(dimension_semantics=sem, vmem_limit_bytes=VMEM_LIMIT)


def _mod_kernel(c_ref, w_ref, b_ref, o_ref):
    c = c_ref[...]
    ca = c * jax.nn.sigmoid(c)
    rows = jnp.concatenate(_split3(ca), axis=0)
    w_hi, w_lo = _split2(w_ref[0])
    y = jnp.dot(rows, w_hi, preferred_element_type=F32)
    y_lo = jnp.dot(rows[:2 * SUBLANE], w_lo, preferred_element_type=F32)
    o_ref[0] = (y[:SUBLANE] + y[SUBLANE:2 * SUBLANE] + y[2 * SUBLANE:]
                + y_lo[:SUBLANE] + y_lo[SUBLANE:] + b_ref[0])


def _modulation(c_pad, w, b, tn=MOD_COLS):
    L, D, M = w.shape
    return pl.pallas_call(
        _mod_kernel,
        out_shape=jax.ShapeDtypeStruct((L, SUBLANE, M), F32),
        grid=(L, M // tn),
        in_specs=[pl.BlockSpec((SUBLANE, D), lambda l, j: (0, 0)),
                  pl.BlockSpec((1, D, tn), lambda l, j: (l, 0, j)),
                  pl.BlockSpec((1, 1, tn), lambda l, j: (l, 0, j))],
        out_specs=pl.BlockSpec((1, SUBLANE, tn), lambda l, j: (l, 0, j)),
        compiler_params=_params(("parallel", "parallel")),
        name="modulation",
    )(c_pad, w, b.reshape(L, 1, M))


def _rwkv_pre_kernel(has_vres, tiles_per_batch, *refs):
    if has_vres:
        (x_ref, xp_ref, mod_ref, g_ref, mu_ref, wr_ref, wk_ref, wv_ref, w0_ref, w1_ref, w2_ref,
         a0_ref, a1_ref, a2_ref, g1_ref, g2_ref, kk_ref, ka_ref,
         vf_ref, v0_ref, v1_ref, v2_ref,
         r_out, k_out, v_out, lw_out, kk_out, a_out, g_out) = refs
    else:
        (x_ref, xp_ref, mod_ref, g_ref, mu_ref, wr_ref, wk_ref, wv_ref, w0_ref, w1_ref, w2_ref,
         a0_ref, a1_ref, a2_ref, g1_ref, g2_ref, kk_ref, ka_ref,
         r_out, k_out, v_out, lw_out, kk_out, a_out, g_out) = refs
    i = pl.program_id(0)
    shift, scale = mod_ref[0, 0], mod_ref[0, 1]
    g = g_ref[...]
    h = _rms_mod(x_ref[...], g, shift, scale)
    tm = h.shape[0]
    h_last = _rms_mod(xp_ref[SUBLANE - 1:SUBLANE, :], g, shift, scale)
    h_last = jnp.where(i % tiles_per_batch == 0, 0.0, h_last)
    row = lax.broadcasted_iota(jnp.int32, (tm, 1), 0)
    h_prev = jnp.where(row == 0, h_last, pltpu.roll(h, 1, axis=0))
    xx = h_prev - h
    xr, xw, xk, xv, xa, xg = (h + xx * mu_ref[j:j + 1, :] for j in range(6))

    r = _bdot(xr, wr_ref[...])
    k = _bdot(xk, wk_ref[...])
    v = _bdot(xv, wv_ref[...])
    w_log = -jax.nn.softplus(-(w0_ref[...] + _bdot(jnp.tanh(_bdot(xw, w1_ref[...])), w2_ref[...]))) - 0.5
    if has_vres:
        mix = jax.nn.sigmoid(v0_ref[...] + _bdot(_bdot(xv, v1_ref[...]), v2_ref[...]))
        v = v + (vf_ref[...] - v) * mix
    a = jax.nn.sigmoid(a0_ref[...] + _bdot(_bdot(xa, a1_ref[...]), a2_ref[...]))
    gate = _bdot(jax.nn.sigmoid(_bdot(xg, g1_ref[...])), g2_ref[...])

    r_out[...] = r
    k_out[...] = k * (1.0 + (a - 1.0) * ka_ref[...])
    v_out[...] = v
    lw_out[...] = -jnp.exp(w_log)
    kk_out[...] = k * kk_ref[...]
    a_out[...] = a
    g_out[...] = gate


def _rwkv_pre(x, mod, norm_g, p, v_first, T, tm=TOKEN_TILE):
    N, D = x.shape
    tpb = T // tm
    has_vres = v_first is not None
    tile = pl.BlockSpec((tm, D), lambda i: (i, 0))
    prev = pl.BlockSpec((SUBLANE, D), lambda i: (jnp.maximum(i * (tm // SUBLANE) - 1, 0), 0))
    mods = pl.BlockSpec((1, N_MOD, 1, D), lambda i: (i // tpb, 0, 0, 0))
    vec = _const_spec((1, D))
    args = [x, x, mod, norm_g, p["mu"], p["wr"], p["wk"], p["wv"], p["w0"], p["w1"], p["w2"],
            p["a0"], p["a1"], p["a2"], p["g1"], p["g2"], p["kk"], p["ka"]]
    specs = [tile, prev, mods, vec, _const_spec(p["mu"].shape)]
    specs += [_const_spec(a.shape) for a in args[5:]]
    if has_vres:
        args += [v_first, p["v0"], p["v1"], p["v2"]]
        specs += [tile, vec, _const_spec(p["v1"].shape), _const_spec(p["v2"].shape)]
    out = jax.ShapeDtypeStruct((N, D), F32)
    return pl.pallas_call(
        functools.partial(_rwkv_pre_kernel, has_vres, tpb),
        out_shape=(out,) * 7,
        grid=(N // tm,),
        in_specs=specs,
        out_specs=(tile,) * 7,
        compiler_params=_params(("parallel",)),
        name="rwkv_pre",
    )(*args)


def _blockdiag(w, diag):
    return jnp.where(diag, jnp.concatenate([w, w], axis=0), jnp.zeros((), w.dtype))


def _pair_mm(l, rs, diag):
    rhs = jnp.concatenate([_blockdiag(r.astype(BF16), diag) for r in rs], axis=1)
    return jnp.dot(l.astype(BF16), rhs, preferred_element_type=F32)


def _pair_mm_nt(l, rs, diag):
    rhs = jnp.concatenate([_blockdiag(r.astype(BF16), diag) for r in rs], axis=0)
    return lax.dot_general(l.astype(BF16), rhs, (((1,), (1,)), ((), ())), preferred_element_type=F32)


def _pair_mm_tn(l, rs, first):
    rhs = jnp.concatenate([r.astype(BF16) for r in rs], axis=1)
    full = lax.dot_general(l.astype(BF16), rhs, (((0,), (0,)), ((), ())), preferred_element_type=F32)
    return [jnp.where(first, full[:HEAD_DIM, i * LANE:(i + 1) * LANE], full[HEAD_DIM:, i * LANE:(i + 1) * LANE])
            for i in range(len(rs))]


def _pair_masks(C):
    row = lax.broadcasted_iota(jnp.int32, (C, LANE), 0)
    lane = lax.broadcasted_iota(jnp.int32, (C, LANE), 1)
    col = lane & (HEAD_DIM - 1)
    first = lane < HEAD_DIM
    row2 = lax.broadcasted_iota(jnp.int32, (LANE, LANE), 0)
    lane2 = lax.broadcasted_iota(jnp.int32, (LANE, LANE), 1)
    diag = (row2 < HEAD_DIM) == (lane2 < HEAD_DIM)
    return row, col, first, diag


def _pair_sum(x, first):
    s0 = jnp.sum(jnp.where(first, x, 0.0), axis=-1, keepdims=True)
    s1 = jnp.sum(jnp.where(first, 0.0, x), axis=-1, keepdims=True)
    return jnp.where(first, s0, s1)


def _wkv_prep_kernel(r_ref, k_ref, v_ref, lw_ref, kk_ref, a_ref, rk_ref,
                     m_out, n_out, rp_out, yl_out, bonus_out):
    C = WKV_CHUNK
    chunks = [slice(ci * C, (ci + 1) * C) for ci in range(r_ref.shape[0] // C)]
    row, col, first, diag = _pair_masks(C)
    tri_incl, tri_strict, eye = row >= col, row > col, row == col
    cum_op = (lax.broadcasted_iota(jnp.int32, (C, C), 0) >= lax.broadcasted_iota(jnp.int32, (C, C), 1)).astype(F32)

    def same_block(s):
        return (row ^ col) < s
    levels, s = [], 8
    while s < C:
        levels.append((same_block(s), same_block(2 * s)))
        s *= 2

    at, rt, bt, kt, bh, kh, w_end, vs = [], [], [], [], [], [], [], []
    for rows in chunks:
        lw, kkr = lw_ref[rows, :] * LOG2E, kk_ref[rows, :]
        kk = kkr / jnp.maximum(jnp.sqrt(_pair_sum(kkr * kkr, first)), L2_EPS)
        bvec = kk * a_ref[rows, :]
        cw = _dot_sel_lhs(cum_op, lw)
        cw_end = cw[C - 1:C, :]
        w_inv, w_rest = jnp.exp2(-cw), jnp.exp2(cw_end - cw)
        k = k_ref[rows, :]
        at.append(-kk * jnp.exp2(cw - lw))
        rt.append(r_ref[rows, :] * jnp.exp2(cw))
        bt.append(bvec * w_inv)
        kt.append(k * w_inv)
        bh.append(bvec * w_rest)
        kh.append(k * w_rest)
        w_end.append(jnp.exp2(cw_end))
        vs.append(v_ref[rows, :])
    n = range(len(chunks))

    amat = [_pair_mm_nt(jnp.concatenate([at[i], rt[i]], axis=0), [bt[i], kt[i]], diag) for i in n]
    a_ab = [jnp.where(tri_strict, amat[i][:C, :LANE], 0.0) for i in n]
    a_ak = [jnp.where(tri_strict, amat[i][:C, LANE:], 0.0) for i in n]
    a_rb = [jnp.where(tri_incl, amat[i][C:, :LANE], 0.0) for i in n]
    a_rk = [jnp.where(tri_incl, amat[i][C:, LANE:], 0.0) for i in n]
    akv = [_pair_mm(a_ak[i], [vs[i]], diag) for i in n]

    blk8 = same_block(8)
    x = [jnp.where(blk8, a_ab[i], 0.0) for i in n]
    inv = [jnp.where(eye, 1.0, x[i]) for i in n]
    for _ in range(2):
        x = [_pair_mm(x[i], [x[i]], diag) for i in n]
        inv = [inv[i] + _pair_mm(x[i], [inv[i]], diag) for i in n]
    for inner, outer in levels:
        off = [jnp.where(outer, jnp.where(inner, 0.0, a_ab[i]), 0.0) for i in n]
        t = [_pair_mm(off[i], [inv[i]], diag) for i in n]
        inv = [inv[i] + _pair_mm(inv[i], [t[i]], diag) for i in n]

    sol = [_pair_mm(inv[i], [at[i], akv[i]], diag) for i in n]
    ap = [sol[i][:, :LANE] for i in n]
    ul = [sol[i][:, LANE:] for i in n]
    for i in n:
        rows = chunks[i]
        rb = _pair_mm(a_rb[i], [ap[i], ul[i]], diag)
        rp_out[rows, :] = rt[i] + rb[:, :LANE]
        yl_out[rows, :] = rb[:, LANE:] + _pair_mm(a_rk[i], [vs[i]], diag)
        bh_ap, bh_ul = _pair_mm_tn(bh[i], [ap[i], ul[i]], first)
        m_out[rows, :] = jnp.where(eye, w_end[i], 0.0) + bh_ap
        n_out[rows, :] = bh_ul + _pair_mm_tn(kh[i], [vs[i]], first)[0]
        bonus_out[rows, :] = _pair_sum(r_ref[rows, :] * k_ref[rows, :] * rk_ref[...], first) * vs[i]


def _wkv_prep(r, k, v, lw, kk, a, rk, tb=WKV_PREP_TOKENS):
    N, D = r.shape
    assert WKV_CHUNK == HEAD_DIM and 2 * HEAD_DIM == LANE
    blk = pl.BlockSpec((tb, LANE), lambda i, p: (i, p))
    out = jax.ShapeDtypeStruct((N, D), F32)
    return pl.pallas_call(
        _wkv_prep_kernel,
        out_shape=(out,) * 5,
        grid=(N // tb, D // LANE),
        in_specs=[blk] * 6 + [pl.BlockSpec((1, LANE), lambda i, p: (0, p))],
        out_specs=(blk,) * 5,
        compiler_params=_params(("parallel", "parallel")),
        name="wkv_prep",
    )(r, k, v, lw, kk, a, rk)


def _wkv_scan_kernel(m_ref, n_ref, rp_ref, yl_ref, bonus_ref, lnw_ref, lnb_ref, y_out, s_ref):
    C = WKV_CHUNK
    B, tb, D = m_ref.shape
    _, _, _, diag = _pair_masks(C)
    head_ones = diag.astype(F32)
    pairs = [(b, slice(p * LANE, (p + 1) * LANE)) for b in range(B) for p in range(D // LANE)]
    for ci in range(tb // C):
        rows = slice(ci * C, (ci + 1) * C)
        prods = []
        for b, lanes in pairs:
            s_hi, s_lo = _split2(s_ref[b, :, lanes])
            d_hi, d_lo = _blockdiag(s_hi, diag), _blockdiag(s_lo, diag)
            a_hi, a_lo = _split2(jnp.concatenate([rp_ref[b, rows, lanes], m_ref[b, rows, lanes]], axis=0))
            prods.append(jnp.dot(jnp.concatenate([a_hi, a_hi, a_lo], axis=1),
                                 jnp.concatenate([d_hi, d_lo, d_hi], axis=0), preferred_element_type=F32))
        ys = [pr[:C] + yl_ref[b, rows, lanes] for (b, lanes), pr in zip(pairs, prods)]
        for (b, lanes), pr in zip(pairs, prods):
            s_ref[b, :, lanes] = pr[C:] + n_ref[b, rows, lanes]
        ycs = [y - _bdot(y, head_ones) * (1.0 / HEAD_DIM) for y in ys]
        vrs = [_bdot(yc * yc, head_ones) * (1.0 / HEAD_DIM) for yc in ycs]
        for (b, lanes), yc, var in zip(pairs, ycs, vrs):
            yn = yc * lax.rsqrt(var + GN_EPS) * lnw_ref[:, lanes] + lnb_ref[:, lanes]
            y_out[b, rows, lanes] = yn + bonus_ref[b, rows, lanes]


def _wkv_scan(m, n, rp, yl, bonus, lnw, lnb, B, tb=WKV_SCAN_TOKENS):
    N, D = m.shape
    T = N // B
    blk = pl.BlockSpec((B, tb, D), lambda t: (0, t, 0))
    blk3 = pl.BlockSpec((B, tb, D), lambda t: (0, t, 0), pipeline_mode=pl.Buffered(3))
    args = [a.reshape(B, T, D) for a in (m, n, rp, yl, bonus)]

    def outer(m_hbm, n_hbm, rp_hbm, yl_hbm, bonus_hbm, lnw_ref, lnb_ref, y_hbm, s_ref):
        s_ref[...] = jnp.zeros_like(s_ref)

        def step(m_ref, n_ref, rp_ref, yl_ref, bonus_ref, y_out):
            _wkv_scan_kernel(m_ref, n_ref, rp_ref, yl_ref, bonus_ref, lnw_ref, lnb_ref, y_out, s_ref)
        pltpu.emit_pipeline(step, grid=(T // tb,), in_specs=[blk3] * 5, out_specs=[blk])(
            m_hbm, n_hbm, rp_hbm, yl_hbm, bonus_hbm, y_hbm)

    hbm = pl.BlockSpec(memory_space=pl.ANY)
    vec = pl.BlockSpec(memory_space=pltpu.VMEM)
    return pl.pallas_call(
        outer,
        out_shape=jax.ShapeDtypeStruct((B, T, D), F32),
        in_specs=[hbm] * 5 + [vec] * 2,
        out_specs=hbm,
        scratch_shapes=[pltpu.VMEM((B, HEAD_DIM, D), F32)],
        compiler_params=pltpu.CompilerParams(vmem_limit_bytes=VMEM_LIMIT),
        name="wkv_scan",
    )(*args, lnw, lnb).reshape(N, D)


def _post_mlp_kernel(final, channel_major, *refs):
    if final:
        (x_ref, y_ref, gm_ref, mod_ref, wo_ref, g_ref, up_ref, down_ref, fg_ref, o_ref) = refs
    else:
        (x_ref, y_ref, gm_ref, mod_ref, wo_ref, g_ref, up_ref, down_ref, o_ref) = refs
    gt1, sh2, sc2, gt2 = mod_ref[0, 2], mod_ref[0, 3], mod_ref[0, 4], mod_ref[0, 5]
    gated = y_ref[...] * gm_ref[...]
    if channel_major:
        mix = _bdot_tn(gated, wo_ref[...])
    else:
        mix = _bdot(gated, wo_ref[...])
    x = x_ref[...] + (1.0 + gt1) * mix
    h = _rms_mod(x, g_ref[...], sh2, sc2)
    u = jnp.maximum(_bdot(h, up_ref[...]), 0.0)
    x = x + (1.0 + gt2) * _bdot(u * u, down_ref[...])
    if final:
        ms = jnp.mean(x * x, axis=-1, keepdims=True)
        x = x * lax.rsqrt(ms + NORM_EPS) * fg_ref[...]
    o_ref[...] = x


def _post_mlp(x, y, gmul, mod, wo, norm_g, w_up, w_down, T, channel_major, final_g=None, tm=TOKEN_TILE):
    N, D = x.shape
    tpb = T // tm
    final = final_g is not None
    tile = pl.BlockSpec((tm, D), lambda i: (i, 0))
    mixer = pl.BlockSpec((D, tm), lambda i: (0, i)) if channel_major else tile
    mods = pl.BlockSpec((1, N_MOD, 1, D), lambda i: (i // tpb, 0, 0, 0))
    args = [x, y, gmul, mod, wo, norm_g, w_up, w_down]
    specs = [tile, mixer, mixer, mods, _const_spec(wo.shape), _const_spec((1, D)),
             _const_spec(w_up.shape), _const_spec(w_down.shape)]
    if final:
        args.append(final_g)
        specs.append(_const_spec((1, D)))
    return pl.pallas_call(
        functools.partial(_post_mlp_kernel, final, channel_major),
        out_shape=jax.ShapeDtypeStruct((N, D), F32),
        grid=(N // tm,),
        in_specs=specs,
        out_specs=tile,
        compiler_params=_params(("parallel",)),
        name="post_mlp",
    )(*args)


def _shared_kv_kernel(tiles_per_batch, x_ref, mod_ref, g_ref, wk_ref, wvt_ref, wf_ref, fb_ref, kg_ref,
                      ind_ref, indt_ref, place_ref, k_out, vt_out, f_out, kf_out, carry_ref):
    i = pl.program_id(0)

    @pl.when(i % tiles_per_batch == 0)
    def _():
        carry_ref[...] = jnp.zeros_like(carry_ref)

    h = _rms_mod(x_ref[...], g_ref[...], mod_ref[0, 0], mod_ref[0, 1])
    tm = h.shape[0]
    k = _bdot(h, wk_ref[...])
    ss = _dot_sel_rhs(k * k, ind_ref[...])
    ms = _dot_sel_rhs(ss, indt_ref[...]) * (1.0 / HEAD_DIM)
    k_out[...] = (k * lax.rsqrt(ms + NORM_EPS) * kg_ref[...]).astype(BF16)
    vt_out[...] = _bdot_nt(wvt_ref[...], h).astype(BF16)
    log_f = jax.nn.log_sigmoid(_bdot(h, wf_ref[...]) + fb_ref[...])
    row = lax.broadcasted_iota(jnp.int32, (tm, tm), 0)
    col = lax.broadcasted_iota(jnp.int32, (tm, tm), 1)
    cum = _dot_sel_lhs((row >= col).astype(F32), log_f) + carry_ref[...]
    f_out[...] = cum
    carry_ref[...] = cum[tm - 1:tm, :]
    pieces = jnp.concatenate(_split3(cum * LOG2E), axis=1)
    kf_out[...] = jnp.dot(pieces, place_ref[...], preferred_element_type=F32).astype(BF16)


def _bias_placement(H, D):
    heads = LANE // HEAD_DIM
    r = jnp.arange(F_PIECES * LANE, dtype=jnp.int32)
    piece, head = r // LANE, r % LANE
    target = (head // heads) * LANE + (head % heads) * F_PIECES + piece
    hit = (target[:, None] == jnp.arange(D, dtype=jnp.int32)[None, :]) & (head < H)[:, None]
    return hit.astype(BF16)


def _shared_kv(x, mod, norm_g, wk, wvt, wf, fb, kg, ind, indt, T, tm=TOKEN_TILE):
    N, D = x.shape
    tpb = T // tm
    place = _bias_placement(D // HEAD_DIM, D)
    tile = pl.BlockSpec((tm, D), lambda i: (i, 0))
    mods = pl.BlockSpec((1, 2, 1, D), lambda i: (i // tpb, 0, 0, 0))
    return pl.pallas_call(
        functools.partial(_shared_kv_kernel, tpb),
        out_shape=(jax.ShapeDtypeStruct((N, D), BF16), jax.ShapeDtypeStruct((D, N), BF16),
                   jax.ShapeDtypeStruct((N, LANE), F32), jax.ShapeDtypeStruct((N, D), BF16)),
        grid=(N // tm,),
        in_specs=[tile, mods, _const_spec((1, D)), _const_spec(wk.shape), _const_spec(wvt.shape),
                  _const_spec(wf.shape), _const_spec((1, LANE)), _const_spec((1, D)),
                  _const_spec(ind.shape), _const_spec(indt.shape), _const_spec(place.shape)],
        out_specs=(tile, pl.BlockSpec((D, tm), lambda i: (0, i)), pl.BlockSpec((tm, LANE), lambda i: (i, 0)),
                   tile),
        scratch_shapes=[pltpu.VMEM((1, LANE), F32)],
        compiler_params=_params(("arbitrary",)),
        name="shared_kv",
    )(x, mod, norm_g, wk, wvt, wf, fb, kg, ind, indt, place)


def _fox_q_kernel(x_ref, mod_ref, g_ref, wqt_ref, wgt_ref, qg_ref, qt_out, sgt_out):
    h = _rms_mod(x_ref[...], g_ref[...], mod_ref[0, 0], mod_ref[0, 1])
    tm = h.shape[0]
    qt = _bdot_nt(wqt_ref[...], h)
    q3 = qt.reshape(qt.shape[0] // HEAD_DIM, HEAD_DIM, tm)
    q3 = q3 * lax.rsqrt(jnp.mean(q3 * q3, axis=1, keepdims=True) + NORM_EPS)
    qt_out[...] = (q3.reshape(qt.shape) * qg_ref[...] * (HEAD_DIM ** -0.5 * LOG2E)).astype(BF16)
    sgt_out[...] = jax.nn.sigmoid(_bdot_nt(wgt_ref[...], h))


def _fox_q(x, mod, norm_g, wqt, wgt, qg_rep, T):
    N, D = x.shape
    tm = qg_rep.shape[1]
    tpb = T // tm
    tile = pl.BlockSpec((tm, D), lambda i: (i, 0))
    ttile = pl.BlockSpec((D, tm), lambda i: (0, i))
    mods = pl.BlockSpec((1, N_MOD, 1, D), lambda i: (i // tpb, 0, 0, 0))
    return pl.pallas_call(
        _fox_q_kernel,
        out_shape=(jax.ShapeDtypeStruct((D, N), BF16), jax.ShapeDtypeStruct((D, N), F32)),
        grid=(N // tm,),
        in_specs=[tile, mods, _const_spec((1, D)), _const_spec(wqt.shape), _const_spec(wgt.shape),
                  _const_spec((D, tm))],
        out_specs=(ttile, ttile),
        compiler_params=_params(("parallel",)),
        name="fox_q",
    )(x, mod, norm_g, wqt, wgt, qg_rep)


def _fox_attn_kernel(tq, tk, sub, ahead, cross, q_blocks, qt_ref, k_ref, kf_ref, vt_ref, ft_ref, ot_ref,
                     z_ref, acc_ref):
    for blk in range(q_blocks):
        lanes = slice(blk * tq, (blk + 1) * tq)
        _fox_attn_block(tq, tk, sub, ahead, cross, pl.program_id(2) * q_blocks + blk,
                        qt_ref.at[:, lanes], k_ref, kf_ref, vt_ref, ft_ref.at[:, :, lanes],
                        ot_ref.at[:, lanes], z_ref, acc_ref)


def _fox_attn_block(tq, tk, sub, ahead, cross, qi, qt_ref, k_ref, kf_ref, vt_ref, ft_ref, ot_ref, z_ref, acc_ref):
    p = pl.program_id(1)
    heads = LANE // HEAD_DIM

    chan = lax.broadcasted_iota(jnp.int32, (2 * LANE, tq), 0)
    key_in_sub = lax.broadcasted_iota(jnp.int32, (sub, tq), 0)
    query_in_blk = lax.broadcasted_iota(jnp.int32, (sub, tq), 1)
    q = qt_ref[...]
    q2 = jnp.concatenate([q, jnp.zeros_like(q)], axis=0)
    qz = []
    for hh in range(heads):
        own = (chan >= hh * HEAD_DIM) & (chan < (hh + 1) * HEAD_DIM)
        bias = (chan >= LANE + hh * F_PIECES) & (chan < LANE + (hh + 1) * F_PIECES)
        qz.append(jnp.where(own, q2, jnp.where(bias, -1.0, 0.0).astype(q.dtype)))
    f_t = [ft_ref[0, pl.ds(p * heads + hh, 1), :] * LOG2E for hh in range(heads)]
    steps = [(u, hh) for u in range(tk // sub) for hh in range(heads)]

    n_steps, n_slots = len(steps), ahead + 1
    assert n_steps % n_slots == 0 and cross <= ahead <= n_steps

    def scores(j, idx, masked):
        u, hh = steps[idx]
        start = pl.multiple_of(j * tk + u * sub, sub)
        keys = jnp.concatenate([k_ref[pl.ds(start, sub), :], kf_ref[pl.ds(start, sub), :]], axis=1)
        z = jnp.dot(keys, qz[hh], preferred_element_type=F32)
        if masked:
            z = jnp.where(start + key_in_sub <= qi * tq + query_in_blk, z, -jnp.inf)
        z_ref[idx % n_slots] = z

    def trip(tiles, carry, next_tile):
        seq = [(j, idx, masked) for j, masked in tiles for idx in range(n_steps)]
        n_own = len(seq)
        if next_tile is not None:
            seq += [(next_tile[0], idx, next_tile[1]) for idx in range(cross)]
        for g in range(cross, min(ahead, len(seq))):
            scores(*seq[g])
        state = list(carry)
        for g in range(n_own):
            if g + ahead < len(seq):
                scores(*seq[g + ahead])
            j, idx, _ = seq[g]
            u, hh = steps[idx]
            z = z_ref[idx % n_slots]
            m = state[hh]
            m_new = jnp.maximum(m, jnp.max(z, axis=0, keepdims=True) + f_t[hh])
            alpha = jnp.exp2(m - m_new)
            e = jnp.exp2((z - (m_new - f_t[hh])).astype(BF16))
            start = pl.multiple_of(j * tk + u * sub, sub)
            v = vt_ref[hh * HEAD_DIM:(hh + 1) * HEAD_DIM, pl.ds(start, sub)]
            v1 = jnp.concatenate([v, ones_rows], axis=0)
            acc_ref[hh] = alpha * acc_ref[hh] + jnp.dot(v1, e, preferred_element_type=F32)
            state[hh] = m_new
        return tuple(state)

    ones_rows = jnp.ones((DENOM_ROWS, sub), BF16)
    acc_ref[...] = jnp.zeros_like(acc_ref)
    init = jnp.full((1, tq), -jnp.inf, F32)
    n_full = qi * (tq // tk)
    n_diag = tq // tk
    for i in range(cross):
        scores(0, i, True)
    n_pairs = lax.shift_right_logical(n_full, 1)
    carry = lax.fori_loop(
        0, n_pairs,
        lambda i, c: trip([(2 * i, False), (2 * i + 1, False)], c, (2 * i + 2, True)), (init,) * heads)
    carry = lax.fori_loop(2 * n_pairs, n_full, lambda j, c: trip([(j, False)], c, (j + 1, True)), carry)
    for d in range(n_diag):
        carry = trip([(n_full + d, True)], carry, (n_full + d + 1, True) if d + 1 < n_diag else None)
    for hh in range(heads):
        acc = acc_ref[hh]
        ot_ref[hh * HEAD_DIM:(hh + 1) * HEAD_DIM, :] = acc[:HEAD_DIM] / acc[HEAD_DIM:HEAD_DIM + 1]


def _fox_attn(qt, k, kf, vt, ft, B, tq=ATTN_Q, tk=ATTN_K, sub=ATTN_SUB, ahead=ATTN_AHEAD, cross=ATTN_CROSS,
              q_blocks=ATTN_Q_BLOCKS):
    D, N = qt.shape
    T = N // B
    ng = T // (tq * q_blocks)
    qblk = pl.BlockSpec((LANE, tq * q_blocks), lambda b, p, i: (p, b * ng + i))
    keys = pl.BlockSpec((T, LANE), lambda b, p, i: (b, p))
    return pl.pallas_call(
        functools.partial(_fox_attn_kernel, tq, tk, sub, ahead, cross, q_blocks),
        out_shape=jax.ShapeDtypeStruct((D, N), F32),
        grid=(B, D // LANE, ng),
        in_specs=[qblk, keys, keys,
                  pl.BlockSpec((LANE, T), lambda b, p, i: (p, b)),
                  pl.BlockSpec((1, ft.shape[1], tq * q_blocks), lambda b, p, i: (b, 0, i))],
        out_specs=qblk,
        scratch_shapes=[pltpu.VMEM((ahead + 1, sub, tq), F32),
                        pltpu.VMEM((LANE // HEAD_DIM, HEAD_DIM + DENOM_ROWS, tq), F32)],
        compiler_params=_params(("parallel", "parallel", "arbitrary")),
        name="fox_attn",
    )(qt, k, kf, vt, ft)


def kernel(x, c, mod_w, mod_b, norm_mix_g, norm_mlp_g, mlp_up, mlp_down, rw_mu, rw_wr, rw_wk, rw_wv, rw_wo, rw_w0, rw_w1, rw_w2, rw_a0, rw_a1, rw_a2, rw_g1, rw_g2, rw_kk, rw_ka, rw_rk, rw_lnw, rw_lnb, rw_v0, rw_v1, rw_v2, kv_norm_g, kv_mod_w, kv_mod_b, kv_w, kv_fb, kv_kg, fx_wqg, fx_qg, fx_wo, final_g):
    B, T, D = x.shape
    depth = mod_w.shape[0]
    n_a = rw_wr.shape[0]
    H = D // HEAD_DIM
    N = B * T
    assert D % LANE == 0 and B <= SUBLANE and H <= LANE
    assert T % TOKEN_TILE == 0 and T % WKV_SCAN_TOKENS == 0 and N % WKV_PREP_TOKENS == 0
    assert T % (ATTN_Q * ATTN_Q_BLOCKS) == 0 and ATTN_Q % ATTN_K == 0 and ATTN_K % ATTN_SUB == 0
    bf = lambda w: w.astype(BF16)
    row = lambda vct: vct.reshape(1, -1)

    c_pad = jnp.pad(c, ((0, SUBLANE - B), (0, 0)))
    mods = _modulation(c_pad, mod_w, mod_b)[:, :B].reshape(depth, B, N_MOD, 1, D)
    kv_mod = _modulation(c_pad, kv_mod_w[None], kv_mod_b[None])[0, :B].reshape(B, 2, 1, D)

    head_of = jnp.arange(D, dtype=jnp.int32) // HEAD_DIM
    ind = (head_of[:, None] == jnp.arange(LANE, dtype=jnp.int32)[None, :]).astype(F32)
    indt = ind.T

    xf = x.reshape(N, D)
    v_first = None
    k_sh = kf_sh = vt_sh = ft_sh = None
    for i in range(depth):
        if i < n_a:
            p = dict(mu=rw_mu[i], wr=bf(rw_wr[i]), wk=bf(rw_wk[i]), wv=bf(rw_wv[i]),
                     w0=row(rw_w0[i]), w1=bf(rw_w1[i]), w2=bf(rw_w2[i]),
                     a0=row(rw_a0[i]), a1=bf(rw_a1[i]), a2=bf(rw_a2[i]),
                     g1=bf(rw_g1[i]), g2=bf(rw_g2[i]), kk=row(rw_kk[i]), ka=row(rw_ka[i]))
            if i > 0:
                p.update(v0=row(rw_v0[i - 1]), v1=bf(rw_v1[i - 1]), v2=bf(rw_v2[i - 1]))
            r, k, v, lw, kk, a, gate = _rwkv_pre(xf, mods[i], row(norm_mix_g[i]), p,
                                                 v_first if i > 0 else None, T)
            if i == 0:
                v_first = v
            m, n, rp, yl, bonus = _wkv_prep(r, k, v, lw, kk, a, row(rw_rk[i]))
            y = _wkv_scan(m, n, rp, yl, bonus, row(rw_lnw[i]), row(rw_lnb[i]), B)
            gmul, wo = gate, bf(rw_wo[i])
        else:
            j = i - n_a
            if j == 0:
                wf = jnp.pad(kv_w[:, 2 * D:], ((0, 0), (0, LANE - H)))
                fb = jnp.pad(kv_fb, (0, LANE - H)).reshape(1, LANE)
                k_sh, vt_sh, f_sh, kf_sh = _shared_kv(xf, kv_mod, row(kv_norm_g), bf(kv_w[:, :D]),
                                                      bf(kv_w[:, D:2 * D].T), bf(wf), fb,
                                                      row(jnp.tile(kv_kg, H)), ind, indt, T)
                ft_sh = f_sh[:, :H].reshape(B, T, H).transpose(0, 2, 1)
            qg_rep = jnp.broadcast_to(jnp.tile(fx_qg[j], H)[:, None], (D, TOKEN_TILE))
            qt, gmul = _fox_q(xf, mods[i], row(norm_mix_g[i]), bf(fx_wqg[j][:, :D].T), bf(fx_wqg[j][:, D:].T),
                              qg_rep, T)
            y = _fox_attn(qt, k_sh, kf_sh, vt_sh, ft_sh, B)
            wo = bf(fx_wo[j])
        xf = _post_mlp(xf, y, gmul, mods[i], wo, row(norm_mlp_g[i]), bf(mlp_up[i]), bf(mlp_down[i]), T, i >= n_a,
                       final_g=row(final_g) if i == depth - 1 else None)
    return xf.reshape(B, T, D)
```
